```python
import jax
import jax.numpy as jnp
from jax import lax
import numpy as np

D_MODEL = 2048
BATCH = 2
SEQ = 8192
DEPTH = 2
DEC_BATCH = 16
DEC_SEQ = 64
PAST_LEN = 1024

CHUNK = 64
QBLK = 128
EPS = 1e-6
N_MOD = 6
FOX_HEADS = 16
FOX_DH = D_MODEL // FOX_HEADS
FOX_W = FOX_HEADS * FOX_DH
FOX_FB_INIT = 5.0
ML_HEADS = 4
ML_DQK = D_MODEL // (2 * ML_HEADS)
ML_DV = D_MODEL // ML_HEADS
ML_QK_W = ML_HEADS * ML_DQK
ML_V_W = ML_HEADS * ML_DV
ML_FB_INIT = 4.0
ML_IB_INIT = -2.0
D_IN = 3 * FOX_W + FOX_HEADS + 2 * ML_QK_W + 2 * ML_V_W + 2 * ML_HEADS
N_EXPERTS = 16
N_GROUPS = 4
EXPERTS_PER_GROUP = N_EXPERTS // N_GROUPS
TOP_K = 2
D_EXPERT = D_MODEL // 4

kernel_name = 'fox_mlstm_gated_hybrid_grouped_moe_stream_step'


def rms_norm(x, g):
    x = x.astype(jnp.float32)
    return x * lax.rsqrt(jnp.mean(x * x, axis=-1, keepdims=True) + EPS) * g.astype(jnp.float32)


def split_columns(z):
    sizes = [FOX_W, FOX_W, FOX_W, FOX_HEADS, ML_QK_W, ML_QK_W, ML_V_W, ML_HEADS, ML_HEADS, ML_V_W]
    cuts = [int(v) for v in np.cumsum(sizes)[:-1]]
    return jnp.split(z, cuts, axis=-1)


def fox_block(q, k, v, lq, lk, qpos, kpos):
    s = jnp.einsum('bqhd,bkhd->bhqk', q, k) * (FOX_DH ** -0.5)
    s = s + jnp.swapaxes(lq, 1, 2)[..., :, None] - jnp.swapaxes(lk, 1, 2)[..., None, :]
    s = jnp.where(kpos[None, :] <= qpos[:, None], s, -jnp.inf)
    p = jax.nn.softmax(s, axis=-1)
    return jnp.einsum('bhqk,bkhd->bqhd', p, v)


def fox_attention(q, k, v, lq, lk, q_offset):
    b, t, h, dh = q.shape
    kpos = jnp.arange(k.shape[1])
    qpos = q_offset + jnp.arange(t)
    if t > QBLK and t % QBLK == 0:
        nb = t // QBLK
        qb = jnp.swapaxes(q.reshape(b, nb, QBLK, h, dh), 0, 1)
        lqb = jnp.swapaxes(lq.reshape(b, nb, QBLK, h), 0, 1)
        pb = qpos.reshape(nb, QBLK)
        out = lax.map(lambda a: fox_block(a[0], k, v, a[1], lk, a[2], kpos), (qb, lqb, pb))
        return jnp.swapaxes(out, 0, 1).reshape(b, t, h, dh)
    return fox_block(q, k, v, lq, lk, qpos, kpos)


def mlstm_chunk(carry, inp):
    c0, n0, m0 = carry
    q, k, v, logi, logf = inp
    lc = q.shape[1]
    bcum = jnp.swapaxes(jnp.cumsum(logf, axis=1), 1, 2)
    li = jnp.swapaxes(logi, 1, 2)
    causal = jnp.tril(jnp.ones((lc, lc), dtype=bool))
    log_d = jnp.where(causal, bcum[..., :, None] - bcum[..., None, :] + li[..., None, :], -jnp.inf)
    log_inter = bcum + m0[..., None]
    m = jnp.maximum(log_inter, jnp.max(log_d, axis=-1))
    d = jnp.exp(log_d - m[..., None])
    w_inter = jnp.exp(log_inter - m)
    s = jnp.einsum('blhd,bshd->bhls', q, k) * d
    num = jnp.einsum('bhls,bshv->blhv', s, v) + jnp.einsum('blhd,bhdv->blhv', q, c0) * jnp.swapaxes(w_inter, 1, 2)[..., None]
    den = jnp.sum(s, axis=-1) + w_inter * jnp.einsum('blhd,bhd->bhl', q, n0)
    den = jnp.maximum(jnp.abs(den), jnp.exp(-m))
    h = num / jnp.swapaxes(den, 1, 2)[..., None]
    m_end = m[..., -1]
    w_key = jnp.exp(bcum[..., -1:] - bcum + li - m_end[..., None])
    f_tot = jnp.exp(bcum[..., -1] + m0 - m_end)
    c1 = f_tot[..., None, None] * c0 + jnp.einsum('bhs,bshd,bshv->bhdv', w_key, k, v)
    n1 = f_tot[..., None] * n0 + jnp.einsum('bhs,bshd->bhd', w_key, k)
    return (c1, n1, m_end), h


def mlstm_recurrence(q, k, v, logi, logf, c0, n0, m0):
    b, t = q.shape[:2]
    lc = min(CHUNK, t)
    nc = t // lc

    def chunks(a):
        return jnp.swapaxes(a.reshape((b, nc, lc) + a.shape[2:]), 0, 1)

    (c1, n1, m1), h = lax.scan(mlstm_chunk, (c0, n0, m0),
                               (chunks(q), chunks(k), chunks(v), chunks(logi), chunks(logf)))
    h = jnp.swapaxes(h, 0, 1).reshape((b, t) + h.shape[3:])
    return h, c1, n1, m1


def moe_ffn(h, w_router, b_router, w_e_gate, w_e_up, w_e_down):
    s = jax.nn.sigmoid(h @ w_router)
    sb = s + b_router
    grp_score = jnp.sum(lax.top_k(sb.reshape(-1, N_GROUPS, EXPERTS_PER_GROUP), TOP_K)[0], axis=-1)
    grp = jnp.argmax(grp_score, axis=-1)
    in_grp = (jnp.arange(N_EXPERTS) // EXPERTS_PER_GROUP)[None, :] == grp[:, None]
    _, idx = lax.top_k(jnp.where(in_grp, sb, -jnp.inf), TOP_K)
    w_sel = jnp.take_along_axis(s, idx, axis=-1)
    w_sel = w_sel / jnp.sum(w_sel, axis=-1, keepdims=True)
    gate = jnp.sum(jax.nn.one_hot(idx, N_EXPERTS, dtype=jnp.float32) * w_sel[..., None], axis=1)
    out = jnp.zeros_like(h)
    for e in range(N_EXPERTS):
        y = (jax.nn.silu(h @ w_e_gate[e]) * (h @ w_e_up[e])) @ w_e_down[e]
        out = out + gate[:, e:e + 1] * y
    return out


def trunk_layer(x, c, p, fox_past, ml_state, q_offset):
    bsz, t, _ = x.shape
    xf = x.astype(jnp.float32)
    mod = jax.nn.silu(c.astype(jnp.float32)) @ p['w_ada'] + p['b_ada']
    sh1, sc1, g1, sh2, sc2, g2 = [a[:, None, :] for a in jnp.split(mod, N_MOD, axis=-1)]
    h = rms_norm(xf, p['norm1_g']) * (1.0 + sc1) + sh1
    fq, fk, fv, ff, mq, mk, mv, mi, mf, mo = split_columns(h @ p['w_in'])
    fq = rms_norm(fq.reshape(bsz, t, FOX_HEADS, FOX_DH), p['fox_qn_g'])
    fk = rms_norm(fk.reshape(bsz, t, FOX_HEADS, FOX_DH), p['fox_kn_g'])
    fv = fv.reshape(bsz, t, FOX_HEADS, FOX_DH)
    flogf = jax.nn.log_sigmoid(ff + p['fox_fb'])
    if fox_past is None:
        k_all, v_all, lf_all = fk, fv, flogf
    else:
        k_all = jnp.concatenate([fox_past[0].astype(jnp.float32), fk], axis=1)
        v_all = jnp.concatenate([fox_past[1].astype(jnp.float32), fv], axis=1)
        lf_all = jnp.concatenate([fox_past[2].astype(jnp.float32), flogf], axis=1)
    l_all = jnp.cumsum(lf_all, axis=1)
    y_fox = fox_attention(fq, k_all, v_all, l_all[:, -t:], l_all, q_offset).reshape(bsz, t, FOX_W)
    mq = mq.reshape(bsz, t, ML_HEADS, ML_DQK) * (ML_DQK ** -0.5)
    mk = mk.reshape(bsz, t, ML_HEADS, ML_DQK)
    mv = mv.reshape(bsz, t, ML_HEADS, ML_DV)
    logi = mi + p['ml_ib']
    logf = jax.nn.log_sigmoid(mf + p['ml_fb'])
    hm, c1, n1, m1 = mlstm_recurrence(mq, mk, mv, logi, logf, ml_state[0], ml_state[1], ml_state[2])
    hm = rms_norm(hm, p['ml_out_g']) * jax.nn.sigmoid(mo).reshape(bsz, t, ML_HEADS, ML_DV)
    y_ml = hm.reshape(bsz, t, ML_V_W)
    ga, gb = jnp.split(h @ p['w_gate'] + p['b_gate'], 2, axis=-1)
    mix = (jax.nn.sigmoid(ga) * y_fox + jax.nn.sigmoid(gb) * y_ml) @ p['w_out']
    x1 = xf + g1 * mix
    h2 = rms_norm(x1, p['norm2_g']) * (1.0 + sc2) + sh2
    moe = moe_ffn(h2.reshape(bsz * t, D_MODEL), p['w_router'], p['b_router'],
                  p['w_e_gate'], p['w_e_up'], p['w_e_down']).reshape(bsz, t, D_MODEL)
    x2 = x1 + g2 * moe
    dt = x.dtype
    new_state = (fk.astype(dt), fv.astype(dt), flogf.astype(dt), c1.astype(dt), n1.astype(dt), m1.astype(dt))
    return x2.astype(dt), new_state


def setup_inputs(seed: int = 0) -> dict:
    key = jax.random.key(seed)
    ks = jax.random.split(key, 32)

    def nrm(k, shape, scale):
        return scale * jax.random.normal(k, shape, jnp.float32)

    sd = D_MODEL ** -0.5
    return {
        'x_prompt': nrm(ks[0], (BATCH, SEQ, D_MODEL), 1.0),
        'x_sample': nrm(ks[1], (DEC_BATCH, DEC_SEQ, D_MODEL), 1.0),
        'cache_fox_k': nrm(ks[2], (DEPTH, DEC_BATCH, PAST_LEN, FOX_HEADS, FOX_DH), 1.0),
        'cache_fox_v': nrm(ks[3], (DEPTH, DEC_BATCH, PAST_LEN, FOX_HEADS, FOX_DH), 1.0),
        'cache_fox_logf': jax.nn.log_sigmoid(FOX_FB_INIT + nrm(ks[4], (DEPTH, DEC_BATCH, PAST_LEN, FOX_HEADS), 1.0)),
        'state_mlstm_c': nrm(ks[5], (DEPTH, DEC_BATCH, ML_HEADS, ML_DQK, ML_DV), 0.3),
        'state_mlstm_n': nrm(ks[6], (DEPTH, DEC_BATCH, ML_HEADS, ML_DQK), 0.3),
        'state_mlstm_m': nrm(ks[7], (DEPTH, DEC_BATCH, ML_HEADS), 0.5),
        'c_prompt': nrm(ks[8], (BATCH, D_MODEL), 1.0),
        'c_sample': nrm(ks[9], (DEC_BATCH, D_MODEL), 1.0),
        'norm1_g': 1.0 + nrm(ks[10], (DEPTH, D_MODEL), 0.1),
        'norm2_g': 1.0 + nrm(ks[11], (DEPTH, D_MODEL), 0.1),
        'w_ada': nrm(ks[12], (DEPTH, D_MODEL, N_MOD * D_MODEL), 0.5 * sd),
        'b_ada': nrm(ks[13], (DEPTH, N_MOD * D_MODEL), 0.02),
        'w_in': nrm(ks[14], (DEPTH, D_MODEL, D_IN), sd),
        'fox_qn_g': 1.0 + nrm(ks[15], (DEPTH, FOX_DH), 0.1),
        'fox_kn_g': 1.0 + nrm(ks[16], (DEPTH, FOX_DH), 0.1),
        'fox_fb': FOX_FB_INIT + nrm(ks[17], (DEPTH, FOX_HEADS), 0.5),
        'ml_ib': ML_IB_INIT + nrm(ks[18], (DEPTH, ML_HEADS), 0.1),
        'ml_fb': ML_FB_INIT + nrm(ks[19], (DEPTH, ML_HEADS), 0.5),
        'ml_out_g': 1.0 + nrm(ks[20], (DEPTH, ML_HEADS, ML_DV), 0.1),
        'w_gate': nrm(ks[21], (DEPTH, D_MODEL, 2 * D_MODEL), sd),
        'b_gate': nrm(ks[22], (DEPTH, 2 * D_MODEL), 0.02),
        'w_out': nrm(ks[23], (DEPTH, D_MODEL, D_MODEL), sd),
        'w_router': nrm(ks[24], (D_MODEL, N_EXPERTS), sd),
        'b_router': nrm(ks[25], (N_EXPERTS,), 0.01),
        'w_e_gate': nrm(ks[26], (DEPTH, N_EXPERTS, D_MODEL, D_EXPERT), sd),
        'w_e_up': nrm(ks[27], (DEPTH, N_EXPERTS, D_MODEL, D_EXPERT), sd),
        'w_e_down': nrm(ks[28], (DEPTH, N_EXPERTS, D_EXPERT, D_MODEL), D_EXPERT ** -0.5),
    }


def reference(x_prompt, x_sample, cache_fox_k, cache_fox_v, cache_fox_logf, state_mlstm_c, state_mlstm_n,
              state_mlstm_m, c_prompt, c_sample, norm1_g, norm2_g, w_ada, b_ada, w_in, fox_qn_g, fox_kn_g,
              fox_fb, ml_ib, ml_fb, ml_out_g, w_gate, b_gate, w_out, w_router, b_router, w_e_gate, w_e_up,
              w_e_down):
    past = cache_fox_k.shape[2]
    bp = x_prompt.shape[0]
    zero_state = (jnp.zeros((bp, ML_HEADS, ML_DQK, ML_DV), jnp.float32),
                  jnp.zeros((bp, ML_HEADS, ML_DQK), jnp.float32),
                  jnp.zeros((bp, ML_HEADS), jnp.float32))
    xp, xs = x_prompt, x_sample
    outs_p = []
    outs_s = []
    for l in range(DEPTH):
        p = {'norm1_g': norm1_g[l], 'norm2_g': norm2_g[l], 'w_ada': w_ada[l], 'b_ada': b_ada[l],
             'w_in': w_in[l], 'fox_qn_g': fox_qn_g[l], 'fox_kn_g': fox_kn_g[l], 'fox_fb': fox_fb[l],
             'ml_ib': ml_ib[l], 'ml_fb': ml_fb[l], 'ml_out_g': ml_out_g[l], 'w_gate': w_gate[l],
             'b_gate': b_gate[l], 'w_out': w_out[l], 'w_router': w_router, 'b_router': b_router,
             'w_e_gate': w_e_gate[l], 'w_e_up': w_e_up[l], 'w_e_down': w_e_down[l]}
        xp, st_p = trunk_layer(xp, c_prompt, p, None, zero_state, 0)
        ml_past = (state_mlstm_c[l].astype(jnp.float32), state_mlstm_n[l].astype(jnp.float32),
                   state_mlstm_m[l].astype(jnp.float32))
        xs, st_s = trunk_layer(xs, c_sample, p, (cache_fox_k[l], cache_fox_v[l], cache_fox_logf[l]),
                               ml_past, past)
        outs_p.append(st_p)
        outs_s.append(st_s)

    def stack(outs, i):
        return jnp.stack([o[i] for o in outs], axis=0)

    return (xp, xs,
            stack(outs_p, 0), stack(outs_p, 1), stack(outs_p, 2), stack(outs_p, 3), stack(outs_p, 4), stack(outs_p, 5),
            stack(outs_s, 0), stack(outs_s, 1), stack(outs_s, 2), stack(outs_s, 3), stack(outs_s, 4), stack(outs_s, 5))
```

```python
import functools

import jax
import jax.numpy as jnp
from jax import lax
from jax.experimental import pallas as pl
from jax.experimental.pallas import tpu as pltpu

EPS = 1e-6
LANES = 128
VMEM_LIMIT = 56 * 1024 * 1024
BF16 = jnp.bfloat16
F32 = jnp.float32
NEG_INF = float("-inf")


def _dot(a, b):
    return jnp.dot(a, b, preferred_element_type=F32)


def _dot_nt(a, b):
    return lax.dot_general(a, b, (((1,), (1,)), ((), ())), preferred_element_type=F32)


def _split3(x):
    hi = x.astype(BF16)
    r = x - hi.astype(F32)
    mid = r.astype(BF16)
    lo = (r - mid.astype(F32)).astype(BF16)
    return hi, mid, lo


def _lane_select(x, idx):
    rows = lax.broadcasted_iota(jnp.int32, (LANES, LANES), 0)
    onehot = jnp.where(rows == idx, 1.0, 0.0).astype(BF16)
    hi, mid, lo = _split3(x)
    return _dot(hi, onehot) + _dot(mid, onehot) + _dot(lo, onehot)


def _log_sigmoid(x):
    return jnp.minimum(x, 0.0) - jnp.log1p(jnp.exp(-jnp.abs(x)))


def _sigmoid(x):
    return 1.0 / (1.0 + jnp.exp(-x))


def _params(*sem):
    return pltpu.CompilerParams(dimension_semantics=sem, vmem_limit_bytes=VMEM_LIMIT)


def _pick(n_list, cands):
    for c in cands:
        if all(n % c == 0 for n in n_list):
            return c
    raise ValueError(f"no tile in {cands} divides {n_list}")


def _ada_kernel(c_ref, w_ref, b_ref, o_ref):
    c = c_ref[...]
    sc = (c * _sigmoid(c)).astype(BF16)
    o_ref[...] = _dot(sc, w_ref[...].astype(BF16)) + b_ref[...]


def _ada_call(c_all, w_ada, b_ada):
    depth, d, n6 = w_ada.shape
    r = c_all.shape[0]
    tn = _pick([n6], (1024, 512, 256, 128))
    return pl.pallas_call(
        _ada_kernel,
        grid=(depth, n6 // tn),
        in_specs=[pl.BlockSpec((r, d), lambda l, j: (0, 0)),
                  pl.BlockSpec((None, d, tn), lambda l, j: (l, 0, j)),
                  pl.BlockSpec((None, 1, tn), lambda l, j: (l, 0, j))],
        out_specs=pl.BlockSpec((None, r, tn), lambda l, j: (l, 0, j)),
        out_shape=jax.ShapeDtypeStruct((depth, r, n6), F32),
        compiler_params=_params("arbitrary", "arbitrary"),
        name="ada_mod",
    )(c_all, w_ada, b_ada.reshape(depth, 1, n6))


def _norm_mod(x, g, sc, sh, grp):
    tm, d = x.shape
    ms = jnp.mean(x * x, axis=1, keepdims=True)
    xn = x * lax.rsqrt(ms + EPS) * g
    h3 = xn.reshape(tm // grp, grp, d) * (1.0 + sc) + sh
    return h3.reshape(tm, d)


def _head_norm(z, gain, dh):
    outs = []
    for hh in range(z.shape[1] // dh):
        zz = z[:, hh * dh:(hh + 1) * dh]
        ms = jnp.mean(zz * zz, axis=1, keepdims=True)
        outs.append(zz * lax.rsqrt(ms + EPS) * gain[:, hh * dh:(hh + 1) * dh])
    return outs


_SEG_OUTS = {"hn": 1, "hn_split": 3, "split": 3, "bf16": 1, "sig": 1}


def _proj_kernel(*refs, segs, grp, dh, npt, with_small):
    x_ref, g_ref, sc_ref, sh_ref, w_ref, gain_ref = refs[:6]
    pos = 6
    if with_small:
        ws_ref, bs_ref = refs[6:8]
        pos = 8
    h_scr = refs[-1]
    out_refs = refs[pos:-1]
    i = pl.program_id(0)
    j = pl.program_id(1)

    @pl.when(j == 0)
    def _():
        h = _norm_mod(x_ref[...], g_ref[...], sc_ref[...], sh_ref[...], grp).astype(BF16)
        h_scr[...] = h
        if with_small:
            ga_ref, gb_ref = out_refs[-2:]
            zs = _dot(h, ws_ref[...]) + bs_ref[...]
            lane = lax.broadcasted_iota(jnp.int32, (1, LANES), 1)
            ga_ref[...] = jnp.where(lane < with_small, _log_sigmoid(zs[:, :LANES]), 0.0)
            gb_ref[...] = zs[:, LANES:]

    z = _dot(h_scr[...], w_ref[...])
    tn = z.shape[1]
    o = 0
    j0 = 0
    for kind, nt in segs:
        outs = out_refs[o:o + _SEG_OUTS[kind]]
        o += _SEG_OUTS[kind]

        def seg_body(kind=kind, outs=outs):
            if kind in ("hn", "hn_split"):
                parts = _head_norm(z, gain_ref[...], dh)
            elif kind == "sig":
                parts = [_sigmoid(z)]
            else:
                parts = [z]
            w = tn // len(parts)
            if kind in ("hn_split", "split"):
                p_ref, s_ref, b_ref = outs
                for k, part in enumerate(parts):
                    b_ref[:, k * w:(k + 1) * w] = part.astype(BF16)

                @pl.when(i < npt)
                def _():
                    for k, part in enumerate(parts):
                        p_ref[:, k * w:(k + 1) * w] = part

                @pl.when(i >= npt)
                def _():
                    for k, part in enumerate(parts):
                        s_ref[:, k * w:(k + 1) * w] = part
            else:
                for k, part in enumerate(parts):
                    outs[0][:, k * w:(k + 1) * w] = part.astype(BF16)

        pl.when((j >= j0) & (j < j0 + nt))(seg_body)
        j0 += nt


def _proj_call(x, g, sc, sh, w, gain, segs, *, tm, tn, grp, dh, n_prompt, small=None, name):
    n, d = x.shape
    nt_total = w.shape[1] // tn
    npt = n_prompt // tm
    nst = (n - n_prompt) // tm
    gr = tm // grp

    def clipj(j, j0, nt):
        return jnp.clip(j - j0, 0, nt - 1)

    in_specs = [pl.BlockSpec((tm, d), lambda i, j: (i, 0)),
                pl.BlockSpec((1, d), lambda i, j: (0, 0)),
                pl.BlockSpec((gr, 1, d), lambda i, j: (i, 0, 0)),
                pl.BlockSpec((gr, 1, d), lambda i, j: (i, 0, 0)),
                pl.BlockSpec((d, tn), lambda i, j: (0, j)),
                pl.BlockSpec((1, tn), lambda i, j: (0, j))]
    args = [x, g, sc, sh, w, gain]
    n_gate = 0
    if small is not None:
        ws, bs, n_gate = small
        in_specs += [pl.BlockSpec(ws.shape, lambda i, j: (0, 0)),
                     pl.BlockSpec(bs.shape, lambda i, j: (0, 0))]
        args += [ws, bs]

    out_specs, out_shapes = [], []
    j0 = 0
    for kind, nt in segs:
        width = nt * tn
        all_spec = pl.BlockSpec((tm, tn), functools.partial(
            lambda i, j, j0, nt: (i, clipj(j, j0, nt)), j0=j0, nt=nt))
        if kind in ("hn_split", "split"):
            out_specs.append(pl.BlockSpec((tm, tn), functools.partial(
                lambda i, j, j0, nt: (jnp.minimum(i, npt - 1),
                                      jnp.where(i < npt, clipj(j, j0, nt), nt - 1)), j0=j0, nt=nt)))
            out_shapes.append(jax.ShapeDtypeStruct((n_prompt, width), F32))
            out_specs.append(pl.BlockSpec((tm, tn), functools.partial(
                lambda i, j, j0, nt: (jnp.maximum(i - npt, 0),
                                      jnp.where(i >= npt, clipj(j, j0, nt), 0)), j0=j0, nt=nt)))
            out_shapes.append(jax.ShapeDtypeStruct((nst * tm, width), F32))
        out_specs.append(all_spec)
        out_shapes.append(jax.ShapeDtypeStruct((n, width), BF16))
        j0 += nt
    assert j0 == nt_total
    if small is not None:
        for _ in range(2):
            out_specs.append(pl.BlockSpec((tm, LANES), lambda i, j: (i, 0)))
            out_shapes.append(jax.ShapeDtypeStruct((n, LANES), F32))

    return pl.pallas_call(
        functools.partial(_proj_kernel, segs=tuple(segs), grp=grp, dh=dh, npt=npt, with_small=n_gate),
        grid=(n // tm, nt_total),
        in_specs=in_specs,
        out_specs=out_specs,
        out_shape=out_shapes,
        scratch_shapes=[pltpu.VMEM((tm, d), BF16)],
        compiler_params=_params("arbitrary", "arbitrary"),
        name=name,
    )(*args)


def _scan_kernel(ga_ref, gb_ref, init_ref, lcs_ref, bcum_ref, b_ref, carry):
    c = pl.program_id(1)

    @pl.when(c == 0)
    def _():
        carry[...] = init_ref[...]

    a = ga_ref[...]
    ln = a.shape[0]
    r = lax.broadcasted_iota(jnp.int32, (ln, ln), 0)
    cc = lax.broadcasted_iota(jnp.int32, (ln, ln), 1)
    tri = jnp.where(r >= cc, 1.0, 0.0).astype(BF16)
    hi, mid, lo = _split3(a)
    cs = _dot(tri, hi) + _dot(tri, mid) + _dot(tri, lo)
    lcs_ref[...] = cs + carry[...]
    bcum_ref[...] = cs
    b_ref[...] = gb_ref[...] - cs
    carry[...] = carry[...] + cs[ln - 1:ln, :]


def _scan_call(ga, gb, init, chunk, name):
    b, t, _ = ga.shape
    spec = pl.BlockSpec((None, chunk, LANES), lambda bi, c: (bi, c, 0))
    shp = jax.ShapeDtypeStruct((b, t, LANES), F32)
    return pl.pallas_call(
        _scan_kernel,
        grid=(b, t // chunk),
        in_specs=[spec, spec, pl.BlockSpec((None, 1, LANES), lambda bi, c: (bi, 0, 0))],
        out_specs=[spec, spec, spec],
        out_shape=[shp, shp, shp],
        scratch_shapes=[pltpu.VMEM((1, LANES), F32)],
        compiler_params=_params("arbitrary", "arbitrary"),
        name=name,
    )(ga, gb, init)


def _fox_kernel(q_ref, k_ref, v_ref, lcs_ref, lk_ref, o_ref, lq_scr, m_scr, l_scr, acc_scr, *, tb):
    h = pl.program_id(1)
    qi = pl.program_id(2)
    lq_scr[...] = _lane_select(lcs_ref[...], h)
    m_scr[...] = jnp.full(m_scr.shape, NEG_INF, F32)
    l_scr[...] = jnp.zeros(l_scr.shape, F32)
    acc_scr[...] = jnp.zeros(acc_scr.shape, F32)
    q = q_ref[...]

    def step(ki, masked):
        off = pl.multiple_of(ki * tb, tb)
        k = k_ref[pl.ds(off, tb), :]
        v = v_ref[pl.ds(off, tb), :]
        s = _dot_nt(q, k) + lq_scr[:, :1] - lk_ref[ki]
        if masked:
            r = lax.broadcasted_iota(jnp.int32, (tb, tb), 0)
            c = lax.broadcasted_iota(jnp.int32, (tb, tb), 1)
            s = jnp.where(r >= c, s, NEG_INF)
        m_prev = m_scr[...]
        m_new = jnp.maximum(m_prev, jnp.max(s, axis=1, keepdims=True))
        p = jnp.exp(s - m_new[:, :1])
        alpha = jnp.exp(m_prev - m_new)
        l_scr[...] = alpha * l_scr[...] + jnp.sum(p, axis=1, keepdims=True)
        acc_scr[...] = alpha * acc_scr[...] + _dot(p.astype(BF16), v)
        m_scr[...] = m_new

    def body(ki, carry):
        step(ki, False)
        return carry

    lax.fori_loop(0, qi, body, 0)
    step(qi, True)
    o_ref[...] = (acc_scr[...] / l_scr[...]).astype(o_ref.dtype)


def _fox_prompt_call(q, k, v, lcs, lk_rows, *, b, t, fh, dh, tb):
    assert dh == LANES
    n = q.shape[0]
    nq = t // tb
    return pl.pallas_call(
        functools.partial(_fox_kernel, tb=tb),
        grid=(b, fh, nq),
        in_specs=[pl.BlockSpec((tb, dh), lambda bi, h, qi: (bi * nq + qi, h)),
                  pl.BlockSpec((t, dh), lambda bi, h, qi: (bi, h)),
                  pl.BlockSpec((t, dh), lambda bi, h, qi: (bi, h)),
                  pl.BlockSpec((tb, LANES), lambda bi, h, qi: (bi * nq + qi, 0)),
                  pl.BlockSpec((None, nq, 1, tb), lambda bi, h, qi: (bi * fh + h, 0, 0, 0))],
        out_specs=pl.BlockSpec((tb, dh), lambda bi, h, qi: (bi * nq + qi, h)),
        out_shape=jax.ShapeDtypeStruct((n, fh * dh), BF16),
        scratch_shapes=[pltpu.VMEM((tb, LANES), F32)] * 3 + [pltpu.VMEM((tb, dh), F32)],
        compiler_params=_params("arbitrary", "arbitrary", "arbitrary"),
        name="fox_prompt",
    )(q, k, v, lcs, lk_rows)


def _fox_sample_kernel(q_ref, kp_ref, vp_ref, kn_ref, vn_ref, lcs_ref, lkp_ref, lkn_ref, y_any, o_ref):
    del y_any
    h = pl.program_id(1)
    ts = q_ref.shape[0]
    q = q_ref[...]
    lq = _lane_select(lcs_ref[...], h)[:, :1]
    sp = _dot_nt(q, kp_ref[...].astype(BF16)) + lq - lkp_ref[...]
    sn = _dot_nt(q, kn_ref[...]) + lq - lkn_ref[...]
    r = lax.broadcasted_iota(jnp.int32, (ts, ts), 0)
    c = lax.broadcasted_iota(jnp.int32, (ts, ts), 1)
    sn = jnp.where(r >= c, sn, NEG_INF)
    m = jnp.maximum(jnp.max(sp, axis=1, keepdims=True), jnp.max(sn, axis=1, keepdims=True))
    pp = jnp.exp(sp - m)
    pn = jnp.exp(sn - m)
    den = jnp.sum(pp, axis=1, keepdims=True) + jnp.sum(pn, axis=1, keepdims=True)
    acc = _dot(pp.astype(BF16), vp_ref[...].astype(BF16)) + _dot(pn.astype(BF16), vn_ref[...])
    o_ref[...] = (acc / den).astype(o_ref.dtype)


def _fox_sample_call(y, q, kb, vb, cache_k, cache_v, lcs_s, lkp_rows, lkn_rows, *, n_prompt, bs, ts, fh, dh):
    p = cache_k.shape[1]
    r0 = n_prompt // ts
    return pl.pallas_call(
        _fox_sample_kernel,
        grid=(bs, fh),
        in_specs=[pl.BlockSpec((ts, dh), lambda bi, h: (r0 + bi, h)),
                  pl.BlockSpec((None, p, dh), lambda bi, h: (bi, 0, h)),
                  pl.BlockSpec((None, p, dh), lambda bi, h: (bi, 0, h)),
                  pl.BlockSpec((ts, dh), lambda bi, h: (r0 + bi, h)),
                  pl.BlockSpec((ts, dh), lambda bi, h: (r0 + bi, h)),
                  pl.BlockSpec((ts, LANES), lambda bi, h: (bi, 0)),
                  pl.BlockSpec((None, 1, p), lambda bi, h: (bi * fh + h, 0, 0)),
                  pl.BlockSpec((None, 1, ts), lambda bi, h: (bi * fh + h, 0, 0)),
                  pl.BlockSpec(memory_space=pl.ANY)],
        out_specs=pl.BlockSpec((ts, dh), lambda bi, h: (r0 + bi, h)),
        out_shape=jax.ShapeDtypeStruct(y.shape, y.dtype),
        input_output_aliases={8: 0},
        compiler_params=_params("arbitrary", "arbitrary"),
        name="fox_sample",
    )(q, cache_k, cache_v, kb, vb, lcs_s, lkp_rows, lkn_rows, y)


def _mlstm_kernel(*refs, hoff, aliased):
    if aliased:
        refs = refs[:11] + refs[12:]
    (q_ref, k_ref, v_ref, o_ref, bcum_ref, bnat_ref, brow_ref, g_ref, c0_ref, n0_ref, m0_ref,
     y_ref, c1_ref, n1_ref, m1_ref, c_scr, n_scr, m_scr) = refs
    h = pl.program_id(1)
    c = pl.program_id(2)
    nc = pl.num_programs(2)

    @pl.when(c == 0)
    def _():
        c_scr[...] = c0_ref[...]
        n_scr[...] = n0_ref[...]
        m_scr[...] = m0_ref[...]

    q = q_ref[...]
    k = k_ref[...]
    v = v_ref[...]
    ln = q.shape[0]
    bcum = _lane_select(bcum_ref[...], hoff + h)[:, :1]
    bcol = _lane_select(bnat_ref[...], hoff + h)[:, :1]
    brow = brow_ref[...]
    m0 = m_scr[:, :1]
    r = lax.broadcasted_iota(jnp.int32, (ln, ln), 0)
    cc = lax.broadcasted_iota(jnp.int32, (ln, ln), 1)
    bm = jnp.where(r >= cc, brow, NEG_INF)
    g = jnp.maximum(m0, jnp.max(bm, axis=1, keepdims=True))
    dmat = jnp.exp(bm - g)
    s = _dot_nt(q, k) * dmat
    w_inter = jnp.exp(m0 - g)
    cmat = c_scr[...]
    num = _dot(s.astype(BF16), v) + _dot(q, cmat.astype(BF16)) * w_inter
    qn = jnp.sum(q.astype(F32) * n_scr[...], axis=1, keepdims=True)
    den = jnp.sum(s, axis=1, keepdims=True) + w_inter * qn
    den = jnp.maximum(jnp.abs(den), jnp.exp(-(bcum + g)))
    hval = num * (1.0 / den)
    ms = jnp.mean(hval * hval, axis=1, keepdims=True)
    y = hval * lax.rsqrt(ms + EPS) * g_ref[...] * o_ref[...].astype(F32)
    y_ref[...] = y.astype(y_ref.dtype)

    g_last = g[ln - 1:ln, :]
    w_key = jnp.exp(bcol - g_last)
    f_tot = jnp.exp(m0 - g_last)
    kw = k.astype(F32) * w_key
    c_new = f_tot * cmat + _dot(kw.T.astype(BF16), v)
    n_new = f_tot * n_scr[...] + jnp.sum(kw, axis=0, keepdims=True)
    m_new = jnp.broadcast_to(bcum[ln - 1:ln, :] + g_last, m_scr.shape)
    c_scr[...] = c_new
    n_scr[...] = n_new
    m_scr[...] = m_new

    @pl.when(c == nc - 1)
    def _():
        c1_ref[...] = c_new
        n1_ref[...] = n_new
        m1_ref[...] = m_new


def _mlstm_call(mq, mk, mv, mo, bcum, bnat, brows, gain, c0, n0, m0, *, row0, nb, nc, chunk, mh, dqk, dv,
                hoff, y_prev, name):
    n = mq.shape[0]
    rb0 = row0 // chunk

    def rows(bi, h, c):
        return rb0 + bi * nc + c

    in_specs = [pl.BlockSpec((chunk, dqk), lambda bi, h, c: (rows(bi, h, c), h)),
                pl.BlockSpec((chunk, dqk), lambda bi, h, c: (rows(bi, h, c), h)),
                pl.BlockSpec((chunk, dv), lambda bi, h, c: (rows(bi, h, c), h)),
                pl.BlockSpec((chunk, dv), lambda bi, h, c: (rows(bi, h, c), h)),
                pl.BlockSpec((chunk, LANES), lambda bi, h, c: (bi * nc + c, 0)),
                pl.BlockSpec((chunk, LANES), lambda bi, h, c: (bi * nc + c, 0)),
                pl.BlockSpec((None, None, 1, chunk), lambda bi, h, c: (bi * mh + h, c, 0, 0)),
                pl.BlockSpec((None, 1, dv), lambda bi, h, c: (h, 0, 0)),
                pl.BlockSpec((None, None, dqk, dv), lambda bi, h, c: (bi, h, 0, 0)),
                pl.BlockSpec((None, None, 1, dqk), lambda bi, h, c: (bi, h, 0, 0)),
                pl.BlockSpec((None, None, 1, LANES), lambda bi, h, c: (bi, h, 0, 0))]
    args = [mq, mk, mv, mo, bcum, bnat, brows, gain, c0, n0, m0]
    aliases = {}
    if y_prev is not None:
        in_specs.append(pl.BlockSpec(memory_space=pl.ANY))
        args.append(y_prev)
        aliases = {len(args) - 1: 0}
    return pl.pallas_call(
        functools.partial(_mlstm_kernel, hoff=hoff, aliased=y_prev is not None),
        grid=(nb, mh, nc),
        in_specs=in_specs,
        out_specs=[pl.BlockSpec((chunk, dv), lambda bi, h, c: (rows(bi, h, c), h)),
                   pl.BlockSpec((None, None, dqk, dv), lambda bi, h, c: (bi, h, 0, 0)),
                   pl.BlockSpec((None, None, 1, dqk), lambda bi, h, c: (bi, h, 0, 0)),
                   pl.BlockSpec((None, None, 1, LANES), lambda bi, h, c: (bi, h, 0, 0))],
        out_shape=[jax.ShapeDtypeStruct((n, mh * dv), BF16),
                   jax.ShapeDtypeStruct((nb, mh, dqk, dv), F32),
                   jax.ShapeDtypeStruct((nb, mh, 1, dqk), F32),
                   jax.ShapeDtypeStruct((nb, mh, 1, LANES), F32)],
        scratch_shapes=[pltpu.VMEM((dqk, dv), F32), pltpu.VMEM((1, dqk), F32), pltpu.VMEM((1, LANES), F32)],
        input_output_aliases=aliases,
        compiler_params=_params("arbitrary", "arbitrary", "arbitrary"),
        name=name,
    )(*args)


def _route(logits, b_router, n_exp, epg):
    n_grp = n_exp // epg
    lane = lax.broadcasted_iota(jnp.int32, logits.shape, 1)
    valid = lane < n_exp
    pos = lane % epg
    grp = lane // epg
    s = _sigmoid(logits)
    sb = jnp.where(valid, s + b_router, NEG_INF)

    def nbr(x, d):
        return pltpu.roll(x, (-d) % LANES, axis=1)

    def count_beaten(x, idx, span, step):
        cnt = jnp.zeros(x.shape, F32)
        for d in range(-(span - 1), span):
            if d == 0:
                continue
            other = nbr(x, d * step)
            inside = (idx + d >= 0) & (idx + d < span)
            ahead = (other > x) | ((other == x) & (d < 0))
            cnt = cnt + jnp.where(inside & ahead, 1.0, 0.0)
        return cnt

    top2 = count_beaten(sb, pos, epg, 1) < 2.0
    t2v = jnp.where(top2, sb, 0.0)
    t2v = jnp.where(valid, t2v, NEG_INF)
    gscore = t2v
    for d in range(-(epg - 1), epg):
        if d == 0:
            continue
        inside = (pos + d >= 0) & (pos + d < epg)
        gscore = gscore + jnp.where(inside, nbr(t2v, d), 0.0)
    best_grp = count_beaten(gscore, grp, n_grp, epg) < 1.0
    sel = valid & best_grp & top2
    w = jnp.where(sel, s, 0.0)
    return w / jnp.sum(w, axis=1, keepdims=True)


def _merge_kernel(x_ref, g1n_ref, sc1_ref, sh1_ref, gt1_ref, yf_ref, ym_ref, wga_ref, wgb_ref, bga_ref, bgb_ref,
                  wo_ref, g2n_ref, sc2_ref, sh2_ref, wrh_ref, wrl_ref, br_ref,
                  x1_ref, h2_ref, gates_ref, h_scr, acc_scr, *, grp, n_exp, epg):
    j = pl.program_id(1)
    nj = pl.num_programs(1)

    @pl.when(j == 0)
    def _():
        h_scr[...] = _norm_mod(x_ref[...], g1n_ref[...], sc1_ref[...], sh1_ref[...], grp).astype(BF16)
        acc_scr[...] = jnp.zeros(acc_scr.shape, F32)

    h = h_scr[...]
    ga = _dot(h, wga_ref[...]) + bga_ref[...]
    gb = _dot(h, wgb_ref[...]) + bgb_ref[...]
    mix = _sigmoid(ga) * yf_ref[...].astype(F32) + _sigmoid(gb) * ym_ref[...].astype(F32)
    acc_scr[...] += _dot(mix.astype(BF16), wo_ref[...])

    @pl.when(j == nj - 1)
    def _():
        tm, d = x_ref.shape
        acc3 = acc_scr[...].reshape(tm // grp, grp, d) * gt1_ref[...]
        x1 = x_ref[...] + acc3.reshape(tm, d)
        x1_ref[...] = x1
        h2 = _norm_mod(x1, g2n_ref[...], sc2_ref[...], sh2_ref[...], grp)
        hi = h2.astype(BF16)
        lo = (h2 - hi.astype(F32)).astype(BF16)
        h2_ref[...] = hi
        logits = _dot(hi, wrh_ref[...]) + _dot(lo, wrh_ref[...]) + _dot(hi, wrl_ref[...])
        gates_ref[...] = _route(logits, br_ref[...], n_exp, epg)


def _merge_call(x, g1n, sc1, sh1, gt1, yf, ym, wgate, bgate, wout, g2n, sc2, sh2, wr_hi, wr_lo, br,
                *, tm, tc, grp, n_exp, epg):
    n, d = x.shape
    nj = d // tc
    gr = tm // grp
    row = pl.BlockSpec((tm, d), lambda i, j: (i, 0))
    vec = pl.BlockSpec((1, d), lambda i, j: (0, 0))
    mod = pl.BlockSpec((gr, 1, d), lambda i, j: (i, 0, 0))
    return pl.pallas_call(
        functools.partial(_merge_kernel, grp=grp, n_exp=n_exp, epg=epg),
        grid=(n // tm, nj),
        in_specs=[row, vec, mod, mod, mod,
                  pl.BlockSpec((tm, tc), lambda i, j: (i, j)),
                  pl.BlockSpec((tm, tc), lambda i, j: (i, j)),
                  pl.BlockSpec((d, tc), lambda i, j: (0, j)),
                  pl.BlockSpec((d, tc), lambda i, j: (0, nj + j)),
                  pl.BlockSpec((1, tc), lambda i, j: (0, j)),
                  pl.BlockSpec((1, tc), lambda i, j: (0, nj + j)),
                  pl.BlockSpec((tc, d), lambda i, j: (j, 0)),
                  vec, mod, mod,
                  pl.BlockSpec((d, LANES), lambda i, j: (0, 0)),
                  pl.BlockSpec((d, LANES), lambda i, j: (0, 0)),
                  pl.BlockSpec((1, LANES), lambda i, j: (0, 0))],
        out_specs=[row, row, pl.BlockSpec((tm, LANES), lambda i, j: (i, 0))],
        out_shape=[jax.ShapeDtypeStruct((n, d), F32), jax.ShapeDtypeStruct((n, d), BF16),
                   jax.ShapeDtypeStruct((n, LANES), F32)],
        scratch_shapes=[pltpu.VMEM((tm, d), BF16), pltpu.VMEM((tm, d), F32)],
        compiler_params=_params("arbitrary", "arbitrary"),
        name="merge_out",
    )(x, g1n, sc1, sh1, gt1, yf, ym, wgate, wgate, bgate, bgate, wout, g2n, sc2, sh2, wr_hi, wr_lo, br)


def _moe_kernel(h_ref, gates_ref, x1_ref, gt2_ref, wg_ref, wu_ref, wd_ref, o_ref, acc_scr, *, grp):
    e = pl.program_id(1)
    ne = pl.num_programs(1)

    @pl.when(e == 0)
    def _():
        acc_scr[...] = jnp.zeros(acc_scr.shape, F32)

    h = h_ref[...]
    gates = gates_ref[...]
    lane = lax.broadcasted_iota(jnp.int32, gates.shape, 1)
    gcol = jnp.sum(jnp.where(lane == e, gates, 0.0), axis=1, keepdims=True)
    a = _dot(h, wg_ref[...])
    u = _dot(h, wu_ref[...])
    y = (a * _sigmoid(a)) * u * gcol
    acc_scr[...] += _dot(y.astype(BF16), wd_ref[...])

    @pl.when(e == ne - 1)
    def _():
        tm, d = o_ref.shape
        acc3 = acc_scr[...].reshape(tm // grp, grp, d) * gt2_ref[...]
        o_ref[...] = x1_ref[...] + acc3.reshape(tm, d)


def _moe_call(h2, gates, x1, gt2, wg, wu, wd, *, tm, grp):
    n, d = x1.shape
    ne, _, de = wg.shape
    gr = tm // grp
    row = pl.BlockSpec((tm, d), lambda i, e: (i, 0))
    return pl.pallas_call(
        functools.partial(_moe_kernel, grp=grp),
        grid=(n // tm, ne),
        in_specs=[row, pl.BlockSpec((tm, LANES), lambda i, e: (i, 0)), row,
                  pl.BlockSpec((gr, 1, d), lambda i, e: (i, 0, 0)),
                  pl.BlockSpec((None, d, de), lambda i, e: (e, 0, 0)),
                  pl.BlockSpec((None, d, de), lambda i, e: (e, 0, 0)),
                  pl.BlockSpec((None, de, d), lambda i, e: (e, 0, 0))],
        out_specs=row,
        out_shape=jax.ShapeDtypeStruct((n, d), F32),
        scratch_shapes=[pltpu.VMEM((tm, d), F32)],
        compiler_params=_params("arbitrary", "arbitrary"),
        name="moe",
    )(h2, gates, x1, gt2, wg, wu, wd)


def kernel(x_prompt, x_sample, cache_fox_k, cache_fox_v, cache_fox_logf, state_mlstm_c, state_mlstm_n, state_mlstm_m, c_prompt, c_sample, norm1_g, norm2_g, w_ada, b_ada, w_in, fox_qn_g, fox_kn_g, fox_fb, ml_ib, ml_fb, ml_out_g, w_gate, b_gate, w_out, w_router, b_router, w_e_gate, w_e_up, w_e_down):
    b, t, d = x_prompt.shape
    bs, ts, _ = x_sample.shape
    depth = w_in.shape[0]
    past = cache_fox_k.shape[2]
    fh = fox_fb.shape[-1]
    dh = d // fh
    fw = fh * dh
    mh = ml_ib.shape[-1]
    dqk, dv = state_mlstm_c.shape[-2:]
    qw, vw = mh * dqk, mh * dv
    n_exp = w_router.shape[-1]
    epg = n_exp // 4
    n_p, n_s = b * t, bs * ts
    n = n_p + n_s
    grp = ts
    tm_a = _pick([n_p, n_s], (1024, 512, 256, 128))
    tm_b = _pick([n_p, n_s], (512, 256, 128))
    tn = _pick([fw, qw, vw], (512, 256, 128))
    tc = _pick([d], (512, 256, 128))
    tb = _pick([t], (512, 256, 128))
    lp = _pick([t], (256, 128, 64))
    pc = _pick([past], (256, 128, 64))
    assert fh + mh <= LANES and n_exp <= LANES and t % grp == 0 and tm_b % grp == 0

    x = jnp.concatenate([x_prompt.reshape(n_p, d), x_sample.reshape(n_s, d)], axis=0)
    n_seq = b + bs
    r_pad = -(-n_seq // 8) * 8
    c_all = jnp.concatenate([c_prompt, c_sample, jnp.zeros((r_pad - n_seq, d), F32)], axis=0)
    mod = _ada_call(c_all, w_ada, b_ada)
    seq_of_group = jnp.concatenate([jnp.repeat(jnp.arange(b), t // grp), b + jnp.arange(bs)])

    cuts = [0]
    for wdt in (fw, fw, fw, fh, qw, qw, vw, mh, mh, vw):
        cuts.append(cuts[-1] + wdt)
    (c_fq, c_fk, c_fv, c_ff, c_mq, c_mk, c_mv, c_mi, c_mf, c_mo, _) = cuts

    wr_hi = jnp.zeros((d, LANES), F32).at[:, :n_exp].set(w_router)
    wr_lo = (wr_hi - wr_hi.astype(BF16).astype(F32)).astype(BF16)
    wr_hi = wr_hi.astype(BF16)
    br = jnp.zeros((1, LANES), F32).at[0, :n_exp].set(b_router)
    zeros_c = jnp.zeros((b, mh, dqk, dv), F32)
    zeros_n = jnp.zeros((b, mh, 1, dqk), F32)
    zeros_m = jnp.zeros((b, mh, 1, LANES), F32)

    outs_p, outs_s = [], []
    for l in range(depth):
        modl = mod[l][seq_of_group]
        sh1, sc1, gt1, sh2, sc2, gt2 = [modl[:, k * d:(k + 1) * d].reshape(n // grp, 1, d) for k in range(6)]
        wl = w_in[l]
        w1 = wl[:, :c_ff].astype(BF16)
        w2 = jnp.concatenate([wl[:, c_mq:c_mk] * (dqk ** -0.5), wl[:, c_mk:c_mi], wl[:, c_mo:]], axis=1).astype(BF16)
        ws = jnp.zeros((d, 2 * LANES), F32)
        ws = ws.at[:, :fh].set(wl[:, c_ff:c_mq]).at[:, fh:fh + mh].set(wl[:, c_mf:c_mo])
        ws = ws.at[:, LANES + fh:LANES + fh + mh].set(wl[:, c_mi:c_mf]).astype(BF16)
        bsm = jnp.zeros((1, 2 * LANES), F32)
        bsm = bsm.at[0, :fh].set(fox_fb[l]).at[0, fh:fh + mh].set(ml_fb[l])
        bsm = bsm.at[0, LANES + fh:LANES + fh + mh].set(ml_ib[l])
        gain1 = jnp.concatenate([jnp.tile(fox_qn_g[l] * (dh ** -0.5), fh), jnp.tile(fox_kn_g[l], fh),
                                 jnp.ones((fw,), F32)]).reshape(1, 3 * fw)
        g1n = norm1_g[l].reshape(1, d)
        g2n = norm2_g[l].reshape(1, d)

        q_b, k_p, k_s, k_b, v_p, v_s, v_b, g_a, g_b = _proj_call(
            x, g1n, sc1, sh1, w1, gain1, [("hn", fw // tn), ("hn_split", fw // tn), ("split", fw // tn)],
            tm=tm_a, tn=tn, grp=grp, dh=dh, n_prompt=n_p, small=(ws, bsm, fh + mh), name="proj_fox")
        m_q, m_k, m_v, m_o = _proj_call(
            x, g1n, sc1, sh1, w2, jnp.ones((1, w2.shape[1]), F32),
            [("bf16", qw // tn), ("bf16", qw // tn), ("bf16", vw // tn), ("sig", vw // tn)],
            tm=tm_a, tn=tn, grp=grp, dh=dh, n_prompt=n_p, name="proj_mlstm")

        lcs_p, bcum_p, bnat_p = _scan_call(g_a[:n_p].reshape(b, t, LANES), g_b[:n_p].reshape(b, t, LANES),
                                           jnp.zeros((b, 1, LANES), F32), lp, "scan_prompt")
        past_lf = jnp.zeros((bs, past, LANES), F32).at[:, :, :fh].set(cache_fox_logf[l])
        lcs_c, _, _ = _scan_call(past_lf, past_lf, jnp.zeros((bs, 1, LANES), F32), pc, "scan_cache")
        lcs_s, bcum_s, bnat_s = _scan_call(g_a[n_p:].reshape(bs, ts, LANES), g_b[n_p:].reshape(bs, ts, LANES),
                                           lcs_c[:, past - 1:past, :], ts, "scan_sample")

        lk_p = jnp.transpose(lcs_p[:, :, :fh], (0, 2, 1)).reshape(b * fh, t // tb, 1, tb)
        y_fox = _fox_prompt_call(q_b, k_b, v_b, lcs_p.reshape(n_p, LANES), lk_p, b=b, t=t, fh=fh, dh=dh, tb=tb)
        lk_c = jnp.transpose(lcs_c[:, :, :fh], (0, 2, 1)).reshape(bs * fh, 1, past)
        lk_n = jnp.transpose(lcs_s[:, :, :fh], (0, 2, 1)).reshape(bs * fh, 1, ts)
        y_fox = _fox_sample_call(y_fox, q_b, k_b, v_b, cache_fox_k[l].reshape(bs, past, fw),
                                 cache_fox_v[l].reshape(bs, past, fw), lcs_s.reshape(n_s, LANES), lk_c, lk_n,
                                 n_prompt=n_p, bs=bs, ts=ts, fh=fh, dh=dh)

        gain_m = ml_out_g[l].reshape(mh, 1, dv)
        br_p = jnp.transpose(bnat_p[:, :, fh:fh + mh], (0, 2, 1)).reshape(b * mh, t // lp, 1, lp)
        y_ml, c1_p, n1_p, m1_p = _mlstm_call(
            m_q, m_k, m_v, m_o, bcum_p.reshape(n_p, LANES), bnat_p.reshape(n_p, LANES), br_p, gain_m,
            zeros_c, zeros_n, zeros_m, row0=0, nb=b, nc=t // lp, chunk=lp, mh=mh, dqk=dqk, dv=dv, hoff=fh,
            y_prev=None, name="mlstm_prompt")
        br_s = jnp.transpose(bnat_s[:, :, fh:fh + mh], (0, 2, 1)).reshape(bs * mh, 1, 1, ts)
        y_ml, c1_s, n1_s, m1_s = _mlstm_call(
            m_q, m_k, m_v, m_o, bcum_s.reshape(n_s, LANES), bnat_s.reshape(n_s, LANES), br_s, gain_m,
            state_mlstm_c[l], state_mlstm_n[l].reshape(bs, mh, 1, dqk),
            jnp.broadcast_to(state_mlstm_m[l][:, :, None, None], (bs, mh, 1, LANES)),
            row0=n_p, nb=bs, nc=1, chunk=ts, mh=mh, dqk=dqk, dv=dv, hoff=fh, y_prev=y_ml, name="mlstm_sample")

        x1, h2, gates = _merge_call(
            x, g1n, sc1, sh1, gt1, y_fox, y_ml, w_gate[l].astype(BF16), b_gate[l].reshape(1, 2 * d),
            w_out[l].astype(BF16), g2n, sc2, sh2, wr_hi, wr_lo, br, tm=tm_b, tc=tc, grp=grp, n_exp=n_exp, epg=epg)
        x = _moe_call(h2, gates, x1, gt2, w_e_gate[l].astype(BF16), w_e_up[l].astype(BF16),
                      w_e_down[l].astype(BF16), tm=tm_b, grp=grp)

        outs_p.append((k_p.reshape(b, t, fh, dh), v_p.reshape(b, t, fh, dh), g_a[:n_p, :fh].reshape(b, t, fh),
                       c1_p, n1_p.reshape(b, mh, dqk), m1_p[:, :, 0, 0]))
        outs_s.append((k_s.reshape(bs, ts, fh, dh), v_s.reshape(bs, ts, fh, dh), g_a[n_p:, :fh].reshape(bs, ts, fh),
                       c1_s, n1_s.reshape(bs, mh, dqk), m1_s[:, :, 0, 0]))

    def stack(outs, i):
        return jnp.stack([o[i] for o in outs], axis=0)

    return (x[:n_p].reshape(b, t, d), x[n_p:].reshape(bs, ts, d),
            stack(outs_p, 0), stack(outs_p, 1), stack(outs_p, 2), stack(outs_p, 3), stack(outs_p, 4), stack(outs_p, 5),
            stack(outs_s, 0), stack(outs_s, 1), stack(outs_s, 2), stack(outs_s, 3), stack(outs_s, 4), stack(outs_s, 5))
```

```python
import functools

import jax
import jax.numpy as jnp
from jax import lax
from jax.experimental import pallas as pl
from jax.experimental.pallas import tpu as pltpu

EPS = 1e-6
LANES = 128
VMEM_LIMIT = 56 * 1024 * 1024
BF16 = jnp.bfloat16
F32 = jnp.float32
NEG_INF = float("-inf")
LOG2E = 1.4426950408889634


def _dot(a, b):
    return jnp.dot(a, b, preferred_element_type=F32)


def _dot_nt(a, b):
    return lax.dot_general(a, b, (((1,), (1,)), ((), ())), preferred_element_type=F32)


def _split3(x):
    hi = x.astype(BF16)
    r = x - hi.astype(F32)
    mid = r.astype(BF16)
    lo = (r - mid.astype(F32)).astype(BF16)
    return hi, mid, lo


def _lane_select(x, idx):
    rows = lax.broadcasted_iota(jnp.int32, (LANES, LANES), 0)
    onehot = jnp.where(rows == idx, 1.0, 0.0).astype(BF16)
    hi, mid, lo = _split3(x)
    return _dot(hi, onehot) + _dot(mid, onehot) + _dot(lo, onehot)


def _log_sigmoid(x):
    return jnp.minimum(x, 0.0) - jnp.log1p(jnp.exp(-jnp.abs(x)))


def _sigmoid(x):
    return 1.0 / (1.0 + jnp.exp(-x))


def _params(*sem, flags=None):
    return pltpu.CompilerParams(dimension_semantics=sem, vmem_limit_bytes=VMEM_LIMIT, flags=flags)


def _pick(n_list, cands):
    for c in cands:
        if all(n % c == 0 for n in n_list):
            return c
    raise ValueError(f"no tile in {cands} divides {n_list}")


def _ada_kernel(c_ref, w_ref, b_ref, o_ref):
    c = c_ref[...]
    sc = (c * _sigmoid(c)).astype(BF16)
    o_ref[...] = _dot(sc, w_ref[...].astype(BF16)) + b_ref[...]


def _ada_call(c_all, w_ada, b_ada):
    depth, d, n6 = w_ada.shape
    r = c_all.shape[0]
    tn = _pick([n6], (1024, 512, 256, 128))
    return pl.pallas_call(
        _ada_kernel,
        grid=(depth, n6 // tn),
        in_specs=[pl.BlockSpec((r, d), lambda l, j: (0, 0)),
                  pl.BlockSpec((None, d, tn), lambda l, j: (l, 0, j)),
                  pl.BlockSpec((None, 1, tn), lambda l, j: (l, 0, j))],
        out_specs=pl.BlockSpec((None, r, tn), lambda l, j: (l, 0, j)),
        out_shape=jax.ShapeDtypeStruct((depth, r, n6), F32),
        compiler_params=_params("arbitrary", "arbitrary"),
        name="ada_mod",
    )(c_all, w_ada, b_ada.reshape(depth, 1, n6))


def _norm_mod(x, g, sc, sh, grp):
    tm, d = x.shape
    ms = jnp.mean(x * x, axis=1, keepdims=True)
    xn = x * lax.rsqrt(ms + EPS) * g
    h3 = xn.reshape(tm // grp, grp, d) * (1.0 + sc) + sh
    return h3.reshape(tm, d)


def _head_norm(z, gain, dh):
    outs = []
    for hh in range(z.shape[1] // dh):
        zz = z[:, hh * dh:(hh + 1) * dh]
        ms = jnp.mean(zz * zz, axis=1, keepdims=True)
        outs.append(zz * lax.rsqrt(ms + EPS) * gain[:, hh * dh:(hh + 1) * dh])
    return outs


_SEG_OUTS = {"hn": 1, "hn_split": 3, "split": 3, "bf16": 1, "sig": 1}


def _proj_kernel(*refs, segs, grp, dh, npt, with_small):
    x_ref, g_ref, sc_ref, sh_ref, w_ref, gain_ref = refs[:6]
    pos = 6
    if with_small:
        ws_ref, bs_ref = refs[6:8]
        pos = 8
    h_scr = refs[-1]
    out_refs = refs[pos:-1]
    i = pl.program_id(0)
    j = pl.program_id(1)

    @pl.when(j == 0)
    def _():
        h = _norm_mod(x_ref[...], g_ref[...], sc_ref[...], sh_ref[...], grp).astype(BF16)
        h_scr[...] = h
        if with_small:
            ga_ref, gb_ref = out_refs[-2:]
            zs = _dot(h, ws_ref[...]) + bs_ref[...]
            lane = lax.broadcasted_iota(jnp.int32, (1, LANES), 1)
            ga_ref[...] = jnp.where(lane < with_small, _log_sigmoid(zs[:, :LANES]), 0.0)
            gb_ref[...] = zs[:, LANES:]

    z = _dot(h_scr[...], w_ref[...])
    tn = z.shape[1]
    o = 0
    j0 = 0
    for kind, nt in segs:
        outs = out_refs[o:o + _SEG_OUTS[kind]]
        o += _SEG_OUTS[kind]

        def seg_body(kind=kind, outs=outs):
            if kind in ("hn", "hn_split"):
                parts = _head_norm(z, gain_ref[...], dh)
            elif kind == "sig":
                parts = [_sigmoid(z)]
            else:
                parts = [z]
            w = tn // len(parts)
            if kind in ("hn_split", "split"):
                p_ref, s_ref, b_ref = outs
                for k, part in enumerate(parts):
                    b_ref[:, k * w:(k + 1) * w] = part.astype(BF16)

                @pl.when(i < npt)
                def _():
                    for k, part in enumerate(parts):
                        p_ref[:, k * w:(k + 1) * w] = part

                @pl.when(i >= npt)
                def _():
                    for k, part in enumerate(parts):
                        s_ref[:, k * w:(k + 1) * w] = part
            else:
                for k, part in enumerate(parts):
                    outs[0][:, k * w:(k + 1) * w] = part.astype(BF16)

        pl.when((j >= j0) & (j < j0 + nt))(seg_body)
        j0 += nt


def _proj_call(x, g, sc, sh, w, gain, segs, *, tm, tn, grp, dh, n_prompt, small=None, name):
    n, d = x.shape
    nt_total = w.shape[1] // tn
    npt = n_prompt // tm
    nst = (n - n_prompt) // tm
    gr = tm // grp

    def clipj(j, j0, nt):
        return jnp.clip(j - j0, 0, nt - 1)

    in_specs = [pl.BlockSpec((tm, d), lambda i, j: (i, 0)),
                pl.BlockSpec((1, d), lambda i, j: (0, 0)),
                pl.BlockSpec((gr, 1, d), lambda i, j: (i, 0, 0)),
                pl.BlockSpec((gr, 1, d), lambda i, j: (i, 0, 0)),
                pl.BlockSpec((d, tn), lambda i, j: (0, j)),
                pl.BlockSpec((1, tn), lambda i, j: (0, j))]
    args = [x, g, sc, sh, w, gain]
    n_gate = 0
    if small is not None:
        ws, bs, n_gate = small
        in_specs += [pl.BlockSpec(ws.shape, lambda i, j: (0, 0)),
                     pl.BlockSpec(bs.shape, lambda i, j: (0, 0))]
        args += [ws, bs]

    out_specs, out_shapes = [], []
    j0 = 0
    for kind, nt in segs:
        width = nt * tn
        all_spec = pl.BlockSpec((tm, tn), functools.partial(
            lambda i, j, j0, nt: (i, clipj(j, j0, nt)), j0=j0, nt=nt))
        if kind in ("hn_split", "split"):
            out_specs.append(pl.BlockSpec((tm, tn), functools.partial(
                lambda i, j, j0, nt: (jnp.minimum(i, npt - 1),
                                      jnp.where(i < npt, clipj(j, j0, nt), nt - 1)), j0=j0, nt=nt)))
            out_shapes.append(jax.ShapeDtypeStruct((n_prompt, width), F32))
            out_specs.append(pl.BlockSpec((tm, tn), functools.partial(
                lambda i, j, j0, nt: (jnp.maximum(i - npt, 0),
                                      jnp.where(i >= npt, clipj(j, j0, nt), 0)), j0=j0, nt=nt)))
            out_shapes.append(jax.ShapeDtypeStruct((nst * tm, width), F32))
        out_specs.append(all_spec)
        out_shapes.append(jax.ShapeDtypeStruct((n, width), BF16))
        j0 += nt
    assert j0 == nt_total
    if small is not None:
        for _ in range(2):
            out_specs.append(pl.BlockSpec((tm, LANES), lambda i, j: (i, 0)))
            out_shapes.append(jax.ShapeDtypeStruct((n, LANES), F32))

    return pl.pallas_call(
        functools.partial(_proj_kernel, segs=tuple(segs), grp=grp, dh=dh, npt=npt, with_small=n_gate),
        grid=(n // tm, nt_total),
        in_specs=in_specs,
        out_specs=out_specs,
        out_shape=out_shapes,
        scratch_shapes=[pltpu.VMEM((tm, d), BF16)],
        compiler_params=_params("arbitrary", "arbitrary"),
        name=name,
    )(*args)


def _scan_kernel(ga_ref, gb_ref, init_ref, lcs_ref, bcum_ref, b_ref, carry):
    c = pl.program_id(1)

    @pl.when(c == 0)
    def _():
        carry[...] = init_ref[...]

    a = ga_ref[...]
    ln = a.shape[0]
    r = lax.broadcasted_iota(jnp.int32, (ln, ln), 0)
    cc = lax.broadcasted_iota(jnp.int32, (ln, ln), 1)
    tri = jnp.where(r >= cc, 1.0, 0.0).astype(BF16)
    hi, mid, lo = _split3(a)
    cs = _dot(tri, hi) + _dot(tri, mid) + _dot(tri, lo)
    lcs_ref[...] = cs + carry[...]
    bcum_ref[...] = cs
    b_ref[...] = gb_ref[...] - cs
    carry[...] = carry[...] + cs[ln - 1:ln, :]


def _scan_call(ga, gb, init, chunk, name):
    b, t, _ = ga.shape
    spec = pl.BlockSpec((None, chunk, LANES), lambda bi, c: (bi, c, 0))
    shp = jax.ShapeDtypeStruct((b, t, LANES), F32)
    return pl.pallas_call(
        _scan_kernel,
        grid=(b, t // chunk),
        in_specs=[spec, spec, pl.BlockSpec((None, 1, LANES), lambda bi, c: (bi, 0, 0))],
        out_specs=[spec, spec, spec],
        out_shape=[shp, shp, shp],
        scratch_shapes=[pltpu.VMEM((1, LANES), F32)],
        compiler_params=_params("arbitrary", "arbitrary"),
        name=name,
    )(ga, gb, init)


def _bias_lanes(x_row, own):
    x = jnp.broadcast_to(x_row, (LANES, x_row.shape[1])).T
    hi, mid, lo = _split3(x)
    lane = lax.broadcasted_iota(jnp.int32, x.shape, 1)
    base = 0 if own else 3
    const = jnp.where((lane >= 3 - base) & (lane < 6 - base), -1.0 if own else 1.0, 0.0).astype(BF16)
    return jnp.where(lane == base, hi, jnp.where(lane == base + 1, mid, jnp.where(lane == base + 2, lo, const)))


def _fox_kernel(q_ref, k_ref, v_ref, lrow_ref, o_ref, vt_scr, ka_scr, qa_scr, m_scr, l_scr, acc_scr,
                sa_scr, ca_scr, sb_scr, cb_scr, *, tb, nh, dh):
    qi = pl.program_id(2)
    nblk = k_ref.shape[0] // tb

    @pl.when(qi == 0)
    def _():
        def prep(c, carry):
            off = pl.multiple_of(c * tb, tb)
            for hh in range(nh):
                v = v_ref[pl.ds(off, tb), hh * dh:(hh + 1) * dh]
                vt_scr[hh, c] = v.astype(F32).T.astype(BF16)
                ka_scr[hh, c, :, :dh] = k_ref[pl.ds(off, tb), hh * dh:(hh + 1) * dh]
                ka_scr[hh, c, :, dh:] = _bias_lanes(lrow_ref[hh, c], False)
            return carry

        lax.fori_loop(0, nblk, prep, 0)

    for hh in range(nh):
        m_scr[hh] = jnp.full((1, tb), NEG_INF, F32)
        l_scr[hh] = jnp.zeros((1, tb), F32)
        acc_scr[hh] = jnp.zeros((dh, tb), F32)
        qa_scr[hh, :, :dh] = q_ref[:, hh * dh:(hh + 1) * dh]
        qa_scr[hh, :, dh:] = _bias_lanes(lrow_ref[hh, qi], True)

    def scores(ki, s_buf, c_buf, masked):
        for hh in range(nh):
            st = _dot_nt(ka_scr[hh, ki], qa_scr[hh])
            if masked:
                r = lax.broadcasted_iota(jnp.int32, (tb, tb), 0)
                c = lax.broadcasted_iota(jnp.int32, (tb, tb), 1)
                st = jnp.where(r <= c, st, NEG_INF)
            s_buf[hh] = st
            c_buf[hh] = jnp.max(st, axis=0, keepdims=True)

    def accumulate(ki, s_buf, c_buf):
        for hh in range(nh):
            m_prev = m_scr[hh]
            m_new = jnp.maximum(m_prev, c_buf[hh])
            p = jnp.exp2(s_buf[hh] - m_new)
            alpha = jnp.exp2(m_prev - m_new)
            l_scr[hh] = alpha * l_scr[hh] + jnp.sum(p, axis=0, keepdims=True)
            acc_scr[hh] = alpha * acc_scr[hh] + _dot(vt_scr[hh, ki], p.astype(BF16))
            m_scr[hh] = m_new

    scores(qi, sa_scr, ca_scr, True)

    def pair(tstep, carry):
        k1 = qi - 1 - 2 * tstep
        scores(k1, sb_scr, cb_scr, False)
        accumulate(k1 + 1, sa_scr, ca_scr)
        scores(k1 - 1, sa_scr, ca_scr, False)
        accumulate(k1, sb_scr, cb_scr)
        return carry

    lax.fori_loop(0, qi // 2, pair, 0)

    @pl.when(qi % 2 == 1)
    def _():
        scores(0, sb_scr, cb_scr, False)
        accumulate(1, sa_scr, ca_scr)
        accumulate(0, sb_scr, cb_scr)

    @pl.when(qi % 2 == 0)
    def _():
        accumulate(0, sa_scr, ca_scr)

    for hh in range(nh):
        o = acc_scr[hh] * (1.0 / l_scr[hh])
        o_ref[:, hh * dh:(hh + 1) * dh] = o.T.astype(o_ref.dtype)


def _fox_prompt_call(q, k, v, lrows, *, b, t, fh, dh, tb, nh):
    n = q.shape[0]
    nq = t // tb
    hw = nh * dh
    return pl.pallas_call(
        functools.partial(_fox_kernel, tb=tb, nh=nh, dh=dh),
        grid=(b, fh // nh, nq),
        in_specs=[pl.BlockSpec((tb, hw), lambda bi, h, qi: (bi * nq + qi, h)),
                  pl.BlockSpec((t, hw), lambda bi, h, qi: (bi, h)),
                  pl.BlockSpec((t, hw), lambda bi, h, qi: (bi, h)),
                  pl.BlockSpec((nh, nq, 1, tb), lambda bi, h, qi: (bi * (fh // nh) + h, 0, 0, 0))],
        out_specs=pl.BlockSpec((tb, hw), lambda bi, h, qi: (bi * nq + qi, h)),
        out_shape=jax.ShapeDtypeStruct((n, fh * dh), BF16),
        scratch_shapes=[pltpu.VMEM((nh, nq, dh, tb), BF16), pltpu.VMEM((nh, nq, tb, dh + LANES), BF16),
                        pltpu.VMEM((nh, tb, dh + LANES), BF16),
                        pltpu.VMEM((nh, 1, tb), F32), pltpu.VMEM((nh, 1, tb), F32),
                        pltpu.VMEM((nh, dh, tb), F32),
                        pltpu.VMEM((nh, tb, tb), F32), pltpu.VMEM((nh, 1, tb), F32),
                        pltpu.VMEM((nh, tb, tb), F32), pltpu.VMEM((nh, 1, tb), F32)],
        compiler_params=_params("arbitrary", "arbitrary", "arbitrary"),
        name="fox_prompt",
    )(q, k, v, lrows)


def _fox_sample_kernel(q_ref, kp_ref, vp_ref, kn_ref, vn_ref, lcs_ref, lkp_ref, lkn_ref, y_any, o_ref):
    del y_any
    h = pl.program_id(1)
    ts = q_ref.shape[0]
    q = q_ref[...]
    lq = _lane_select(lcs_ref[...], h)[:, :1] * LOG2E
    sp = _dot_nt(q, kp_ref[...].astype(BF16)) + lq - lkp_ref[...]
    sn = _dot_nt(q, kn_ref[...]) + lq - lkn_ref[...]
    r = lax.broadcasted_iota(jnp.int32, (ts, ts), 0)
    c = lax.broadcasted_iota(jnp.int32, (ts, ts), 1)
    sn = jnp.where(r >= c, sn, NEG_INF)
    m = jnp.maximum(jnp.max(sp, axis=1, keepdims=True), jnp.max(sn, axis=1, keepdims=True))
    pp = jnp.exp2(sp - m)
    pn = jnp.exp2(sn - m)
    den = jnp.sum(pp, axis=1, keepdims=True) + jnp.sum(pn, axis=1, keepdims=True)
    acc = _dot(pp.astype(BF16), vp_ref[...].astype(BF16)) + _dot(pn.astype(BF16), vn_ref[...])
    o_ref[...] = (acc / den).astype(o_ref.dtype)


def _fox_sample_call(y, q, kb, vb, cache_k, cache_v, lcs_s, lkp_rows, lkn_rows, *, n_prompt, bs, ts, fh, dh):
    p = cache_k.shape[1]
    r0 = n_prompt // ts
    return pl.pallas_call(
        _fox_sample_kernel,
        grid=(bs, fh),
        in_specs=[pl.BlockSpec((ts, dh), lambda bi, h: (r0 + bi, h)),
                  pl.BlockSpec((None, p, dh), lambda bi, h: (bi, 0, h)),
                  pl.BlockSpec((None, p, dh), lambda bi, h: (bi, 0, h)),
                  pl.BlockSpec((ts, dh), lambda bi, h: (r0 + bi, h)),
                  pl.BlockSpec((ts, dh), lambda bi, h: (r0 + bi, h)),
                  pl.BlockSpec((ts, LANES), lambda bi, h: (bi, 0)),
                  pl.BlockSpec((None, 1, p), lambda bi, h: (bi * fh + h, 0, 0)),
                  pl.BlockSpec((None, 1, ts), lambda bi, h: (bi * fh + h, 0, 0)),
                  pl.BlockSpec(memory_space=pl.ANY)],
        out_specs=pl.BlockSpec((ts, dh), lambda bi, h: (r0 + bi, h)),
        out_shape=jax.ShapeDtypeStruct(y.shape, y.dtype),
        input_output_aliases={8: 0},
        compiler_params=_params("arbitrary", "arbitrary"),
        name="fox_sample",
    )(q, cache_k, cache_v, kb, vb, lcs_s, lkp_rows, lkn_rows, y)


def _mlstm_kernel(*refs, hoff, aliased):
    if aliased:
        refs = refs[:11] + refs[12:]
    (q_ref, k_ref, v_ref, o_ref, bcum_ref, bnat_ref, brow_ref, g_ref, c0_ref, n0_ref, m0_ref,
     y_ref, c1_ref, n1_ref, m1_ref, c_scr, n_scr, m_scr) = refs
    h = pl.program_id(1)
    c = pl.program_id(2)
    nc = pl.num_programs(2)

    @pl.when(c == 0)
    def _():
        c_scr[...] = c0_ref[...]
        n_scr[...] = n0_ref[...]
        m_scr[...] = m0_ref[...]

    q = q_ref[...]
    k = k_ref[...]
    v = v_ref[...]
    ln = q.shape[0]
    bcum = _lane_select(bcum_ref[...], hoff + h)[:, :1]
    bcol = _lane_select(bnat_ref[...], hoff + h)[:, :1]
    brow = brow_ref[...]
    m0 = m_scr[:, :1]
    r = lax.broadcasted_iota(jnp.int32, (ln, ln), 0)
    cc = lax.broadcasted_iota(jnp.int32, (ln, ln), 1)
    bm = jnp.where(r >= cc, brow, NEG_INF)
    g = jnp.maximum(m0, jnp.max(bm, axis=1, keepdims=True))
    dmat = jnp.exp(bm - g)
    s = _dot_nt(q, k) * dmat
    w_inter = jnp.exp(m0 - g)
    cmat = c_scr[...]
    num = _dot(s.astype(BF16), v) + _dot(q, cmat.astype(BF16)) * w_inter
    qn = jnp.sum(q.astype(F32) * n_scr[...], axis=1, keepdims=True)
    den = jnp.sum(s, axis=1, keepdims=True) + w_inter * qn
    den = jnp.maximum(jnp.abs(den), jnp.exp(-(bcum + g)))
    hval = num * (1.0 / den)
    ms = jnp.mean(hval * hval, axis=1, keepdims=True)
    y = hval * lax.rsqrt(ms + EPS) * g_ref[...] * o_ref[...].astype(F32)
    y_ref[...] = y.astype(y_ref.dtype)

    g_last = g[ln - 1:ln, :]
    w_key = jnp.exp(bcol - g_last)
    f_tot = jnp.exp(m0 - g_last)
    kw = k.astype(F32) * w_key
    c_new = f_tot * cmat + _dot(kw.T.astype(BF16), v)
    n_new = f_tot * n_scr[...] + jnp.sum(kw, axis=0, keepdims=True)
    m_new = jnp.broadcast_to(bcum[ln - 1:ln, :] + g_last, m_scr.shape)
    c_scr[...] = c_new
    n_scr[...] = n_new
    m_scr[...] = m_new

    @pl.when(c == nc - 1)
    def _():
        c1_ref[...] = c_new
        n1_ref[...] = n_new
        m1_ref[...] = m_new


def _mlstm_call(mq, mk, mv, mo, bcum, bnat, brows, gain, c0, n0, m0, *, row0, nb, nc, chunk, mh, dqk, dv,
                hoff, y_prev, name):
    n = mq.shape[0]
    rb0 = row0 // chunk

    def rows(bi, h, c):
        return rb0 + bi * nc + c

    in_specs = [pl.BlockSpec((chunk, dqk), lambda bi, h, c: (rows(bi, h, c), h)),
                pl.BlockSpec((chunk, dqk), lambda bi, h, c: (rows(bi, h, c), h)),
                pl.BlockSpec((chunk, dv), lambda bi, h, c: (rows(bi, h, c), h)),
                pl.BlockSpec((chunk, dv), lambda bi, h, c: (rows(bi, h, c), h)),
                pl.BlockSpec((chunk, LANES), lambda bi, h, c: (bi * nc + c, 0)),
                pl.BlockSpec((chunk, LANES), lambda bi, h, c: (bi * nc + c, 0)),
                pl.BlockSpec((None, None, 1, chunk), lambda bi, h, c: (bi * mh + h, c, 0, 0)),
                pl.BlockSpec((None, 1, dv), lambda bi, h, c: (h, 0, 0)),
                pl.BlockSpec((None, None, dqk, dv), lambda bi, h, c: (bi, h, 0, 0)),
                pl.BlockSpec((None, None, 1, dqk), lambda bi, h, c: (bi, h, 0, 0)),
                pl.BlockSpec((None, None, 1, LANES), lambda bi, h, c: (bi, h, 0, 0))]
    args = [mq, mk, mv, mo, bcum, bnat, brows, gain, c0, n0, m0]
    aliases = {}
    if y_prev is not None:
        in_specs.append(pl.BlockSpec(memory_space=pl.ANY))
        args.append(y_prev)
        aliases = {len(args) - 1: 0}
    return pl.pallas_call(
        functools.partial(_mlstm_kernel, hoff=hoff, aliased=y_prev is not None),
        grid=(nb, mh, nc),
        in_specs=in_specs,
        out_specs=[pl.BlockSpec((chunk, dv), lambda bi, h, c: (rows(bi, h, c), h)),
                   pl.BlockSpec((None, None, dqk, dv), lambda bi, h, c: (bi, h, 0, 0)),
                   pl.BlockSpec((None, None, 1, dqk), lambda bi, h, c: (bi, h, 0, 0)),
                   pl.BlockSpec((None, None, 1, LANES), lambda bi, h, c: (bi, h, 0, 0))],
        out_shape=[jax.ShapeDtypeStruct((n, mh * dv), BF16),
                   jax.ShapeDtypeStruct((nb, mh, dqk, dv), F32),
                   jax.ShapeDtypeStruct((nb, mh, 1, dqk), F32),
                   jax.ShapeDtypeStruct((nb, mh, 1, LANES), F32)],
        scratch_shapes=[pltpu.VMEM((dqk, dv), F32), pltpu.VMEM((1, dqk), F32), pltpu.VMEM((1, LANES), F32)],
        input_output_aliases=aliases,
        compiler_params=_params("arbitrary", "arbitrary", "arbitrary"),
        name=name,
    )(*args)


def _route(logits, b_router, n_exp, epg):
    n_grp = n_exp // epg
    lane = lax.broadcasted_iota(jnp.int32, logits.shape, 1)
    valid = lane < n_exp
    pos = lane % epg
    grp = lane // epg
    s = _sigmoid(logits)
    sb = jnp.where(valid, s + b_router, NEG_INF)

    def nbr(x, d):
        return pltpu.roll(x, (-d) % LANES, axis=1)

    def count_beaten(x, idx, span, step):
        cnt = jnp.zeros(x.shape, F32)
        for d in range(-(span - 1), span):
            if d == 0:
                continue
            other = nbr(x, d * step)
            inside = (idx + d >= 0) & (idx + d < span)
            ahead = (other > x) | ((other == x) & (d < 0))
            cnt = cnt + jnp.where(inside & ahead, 1.0, 0.0)
        return cnt

    top2 = count_beaten(sb, pos, epg, 1) < 2.0
    t2v = jnp.where(top2, sb, 0.0)
    t2v = jnp.where(valid, t2v, NEG_INF)
    gscore = t2v
    for d in range(-(epg - 1), epg):
        if d == 0:
            continue
        inside = (pos + d >= 0) & (pos + d < epg)
        gscore = gscore + jnp.where(inside, nbr(t2v, d), 0.0)
    best_grp = count_beaten(gscore, grp, n_grp, epg) < 1.0
    sel = valid & best_grp & top2
    w = jnp.where(sel, s, 0.0)
    return w / jnp.sum(w, axis=1, keepdims=True)


def _merge_kernel(x_ref, g1n_ref, sc1_ref, sh1_ref, gt1_ref, yf_ref, ym_ref, wga_ref, wgb_ref, bga_ref, bgb_ref,
                  wo_ref, g2n_ref, sc2_ref, sh2_ref, wrh_ref, wrl_ref, br_ref,
                  x1_ref, h2_ref, gates_ref, h_scr, acc_scr, *, grp, n_exp, epg):
    j = pl.program_id(1)
    nj = pl.num_programs(1)

    @pl.when(j == 0)
    def _():
        h_scr[...] = _norm_mod(x_ref[...], g1n_ref[...], sc1_ref[...], sh1_ref[...], grp).astype(BF16)
        acc_scr[...] = jnp.zeros(acc_scr.shape, F32)

    h = h_scr[...]
    ga = _dot(h, wga_ref[...]) + bga_ref[...]
    gb = _dot(h, wgb_ref[...]) + bgb_ref[...]
    mix = _sigmoid(ga) * yf_ref[...].astype(F32) + _sigmoid(gb) * ym_ref[...].astype(F32)
    acc_scr[...] += _dot(mix.astype(BF16), wo_ref[...])

    @pl.when(j == nj - 1)
    def _():
        tm, d = x_ref.shape
        acc3 = acc_scr[...].reshape(tm // grp, grp, d) * gt1_ref[...]
        x1 = x_ref[...] + acc3.reshape(tm, d)
        x1_ref[...] = x1
        h2 = _norm_mod(x1, g2n_ref[...], sc2_ref[...], sh2_ref[...], grp)
        hi = h2.astype(BF16)
        lo = (h2 - hi.astype(F32)).astype(BF16)
        h2_ref[...] = hi
        logits = _dot(hi, wrh_ref[...]) + _dot(lo, wrh_ref[...]) + _dot(hi, wrl_ref[...])
        gates_ref[...] = _route(logits, br_ref[...], n_exp, epg)


def _merge_call(x, g1n, sc1, sh1, gt1, yf, ym, wgate, bgate, wout, g2n, sc2, sh2, wr_hi, wr_lo, br,
                *, tm, tc, grp, n_exp, epg):
    n, d = x.shape
    nj = d // tc
    gr = tm // grp
    row = pl.BlockSpec((tm, d), lambda i, j: (i, 0))
    vec = pl.BlockSpec((1, d), lambda i, j: (0, 0))
    mod = pl.BlockSpec((gr, 1, d), lambda i, j: (i, 0, 0))
    return pl.pallas_call(
        functools.partial(_merge_kernel, grp=grp, n_exp=n_exp, epg=epg),
        grid=(n // tm, nj),
        in_specs=[row, vec, mod, mod, mod,
                  pl.BlockSpec((tm, tc), lambda i, j: (i, j)),
                  pl.BlockSpec((tm, tc), lambda i, j: (i, j)),
                  pl.BlockSpec((d, tc), lambda i, j: (0, j)),
                  pl.BlockSpec((d, tc), lambda i, j: (0, nj + j)),
                  pl.BlockSpec((1, tc), lambda i, j: (0, j)),
                  pl.BlockSpec((1, tc), lambda i, j: (0, nj + j)),
                  pl.BlockSpec((tc, d), lambda i, j: (j, 0)),
                  vec, mod, mod,
                  pl.BlockSpec((d, LANES), lambda i, j: (0, 0)),
                  pl.BlockSpec((d, LANES), lambda i, j: (0, 0)),
                  pl.BlockSpec((1, LANES), lambda i, j: (0, 0))],
        out_specs=[row, row, pl.BlockSpec((tm, LANES), lambda i, j: (i, 0))],
        out_shape=[jax.ShapeDtypeStruct((n, d), F32), jax.ShapeDtypeStruct((n, d), BF16),
                   jax.ShapeDtypeStruct((n, LANES), F32)],
        scratch_shapes=[pltpu.VMEM((tm, d), BF16), pltpu.VMEM((tm, d), F32)],
        compiler_params=_params("arbitrary", "arbitrary"),
        name="merge_out",
    )(x, g1n, sc1, sh1, gt1, yf, ym, wgate, wgate, bgate, bgate, wout, g2n, sc2, sh2, wr_hi, wr_lo, br)


def _moe_kernel(h_ref, gates_ref, x1_ref, gt2_ref, wg_ref, wu_ref, wd_ref, o_ref, acc_scr, *, grp):
    e = pl.program_id(1)
    ne = pl.num_programs(1)

    @pl.when(e == 0)
    def _():
        acc_scr[...] = jnp.zeros(acc_scr.shape, F32)

    h = h_ref[...]
    gates = gates_ref[...]
    lane = lax.broadcasted_iota(jnp.int32, gates.shape, 1)
    gcol = jnp.sum(jnp.where(lane == e, gates, 0.0), axis=1, keepdims=True)
    a = _dot(h, wg_ref[...])
    u = _dot(h, wu_ref[...])
    y = (a * _sigmoid(a)) * u * gcol
    acc_scr[...] += _dot(y.astype(BF16), wd_ref[...])

    @pl.when(e == ne - 1)
    def _():
        tm, d = o_ref.shape
        acc3 = acc_scr[...].reshape(tm // grp, grp, d) * gt2_ref[...]
        o_ref[...] = x1_ref[...] + acc3.reshape(tm, d)


def _moe_call(h2, gates, x1, gt2, wg, wu, wd, *, tm, grp):
    n, d = x1.shape
    ne, _, de = wg.shape
    gr = tm // grp
    row = pl.BlockSpec((tm, d), lambda i, e: (i, 0))
    return pl.pallas_call(
        functools.partial(_moe_kernel, grp=grp),
        grid=(n // tm, ne),
        in_specs=[row, pl.BlockSpec((tm, LANES), lambda i, e: (i, 0)), row,
                  pl.BlockSpec((gr, 1, d), lambda i, e: (i, 0, 0)),
                  pl.BlockSpec((None, d, de), lambda i, e: (e, 0, 0)),
                  pl.BlockSpec((None, d, de), lambda i, e: (e, 0, 0)),
                  pl.BlockSpec((None, de, d), lambda i, e: (e, 0, 0))],
        out_specs=row,
        out_shape=jax.ShapeDtypeStruct((n, d), F32),
        scratch_shapes=[pltpu.VMEM((tm, d), F32)],
        compiler_params=_params("arbitrary", "arbitrary"),
        name="moe",
    )(h2, gates, x1, gt2, wg, wu, wd)


def kernel(x_prompt, x_sample, cache_fox_k, cache_fox_v, cache_fox_logf, state_mlstm_c, state_mlstm_n, state_mlstm_m, c_prompt, c_sample, norm1_g, norm2_g, w_ada, b_ada, w_in, fox_qn_g, fox_kn_g, fox_fb, ml_ib, ml_fb, ml_out_g, w_gate, b_gate, w_out, w_router, b_router, w_e_gate, w_e_up, w_e_down):
    b, t, d = x_prompt.shape
    bs, ts, _ = x_sample.shape
    depth = w_in.shape[0]
    past = cache_fox_k.shape[2]
    fh = fox_fb.shape[-1]
    dh = d // fh
    fw = fh * dh
    mh = ml_ib.shape[-1]
    dqk, dv = state_mlstm_c.shape[-2:]
    qw, vw = mh * dqk, mh * dv
    n_exp = w_router.shape[-1]
    epg = n_exp // 4
    n_p, n_s = b * t, bs * ts
    n = n_p + n_s
    grp = ts
    tm_a = _pick([n_p, n_s], (1024, 512, 256, 128))
    tm_b = _pick([n_p, n_s], (512, 256, 128))
    tn = _pick([fw, qw, vw], (512, 256, 128))
    tc = _pick([d], (512, 256, 128))
    tb = _pick([t], (512, 256, 128))
    nh = 2 if fh % 2 == 0 else 1
    lp = _pick([t], (256, 128, 64))
    pc = _pick([past], (256, 128, 64))
    assert fh + mh <= LANES and n_exp <= LANES and t % grp == 0 and tm_b % grp == 0

    x = jnp.concatenate([x_prompt.reshape(n_p, d), x_sample.reshape(n_s, d)], axis=0)
    n_seq = b + bs
    r_pad = -(-n_seq // 8) * 8
    c_all = jnp.concatenate([c_prompt, c_sample, jnp.zeros((r_pad - n_seq, d), F32)], axis=0)
    mod = _ada_call(c_all, w_ada, b_ada)
    seq_of_group = jnp.concatenate([jnp.repeat(jnp.arange(b), t // grp), b + jnp.arange(bs)])

    cuts = [0]
    for wdt in (fw, fw, fw, fh, qw, qw, vw, mh, mh, vw):
        cuts.append(cuts[-1] + wdt)
    (c_fq, c_fk, c_fv, c_ff, c_mq, c_mk, c_mv, c_mi, c_mf, c_mo, _) = cuts

    wr_hi = jnp.zeros((d, LANES), F32).at[:, :n_exp].set(w_router)
    wr_lo = (wr_hi - wr_hi.astype(BF16).astype(F32)).astype(BF16)
    wr_hi = wr_hi.astype(BF16)
    br = jnp.zeros((1, LANES), F32).at[0, :n_exp].set(b_router)
    zeros_c = jnp.zeros((b, mh, dqk, dv), F32)
    zeros_n = jnp.zeros((b, mh, 1, dqk), F32)
    zeros_m = jnp.zeros((b, mh, 1, LANES), F32)

    outs_p, outs_s = [], []
    for l in range(depth):
        modl = mod[l][seq_of_group]
        sh1, sc1, gt1, sh2, sc2, gt2 = [modl[:, k * d:(k + 1) * d].reshape(n // grp, 1, d) for k in range(6)]
        wl = w_in[l]
        w1 = wl[:, :c_ff].astype(BF16)
        w2 = jnp.concatenate([wl[:, c_mq:c_mk] * (dqk ** -0.5), wl[:, c_mk:c_mi], wl[:, c_mo:]], axis=1).astype(BF16)
        ws = jnp.zeros((d, 2 * LANES), F32)
        ws = ws.at[:, :fh].set(wl[:, c_ff:c_mq]).at[:, fh:fh + mh].set(wl[:, c_mf:c_mo])
        ws = ws.at[:, LANES + fh:LANES + fh + mh].set(wl[:, c_mi:c_mf]).astype(BF16)
        bsm = jnp.zeros((1, 2 * LANES), F32)
        bsm = bsm.at[0, :fh].set(fox_fb[l]).at[0, fh:fh + mh].set(ml_fb[l])
        bsm = bsm.at[0, LANES + fh:LANES + fh + mh].set(ml_ib[l])
        gain1 = jnp.concatenate([jnp.tile(fox_qn_g[l] * (LOG2E * dh ** -0.5), fh), jnp.tile(fox_kn_g[l], fh),
                                 jnp.ones((fw,), F32)]).reshape(1, 3 * fw)
        g1n = norm1_g[l].reshape(1, d)
        g2n = norm2_g[l].reshape(1, d)

        q_b, k_p, k_s, k_b, v_p, v_s, v_b, g_a, g_b = _proj_call(
            x, g1n, sc1, sh1, w1, gain1, [("hn", fw // tn), ("hn_split", fw // tn), ("split", fw // tn)],
            tm=tm_a, tn=tn, grp=grp, dh=dh, n_prompt=n_p, small=(ws, bsm, fh + mh), name="proj_fox")
        m_q, m_k, m_v, m_o = _proj_call(
            x, g1n, sc1, sh1, w2, jnp.ones((1, w2.shape[1]), F32),
            [("bf16", qw // tn), ("bf16", qw // tn), ("bf16", vw // tn), ("sig", vw // tn)],
            tm=tm_a, tn=tn, grp=grp, dh=dh, n_prompt=n_p, name="proj_mlstm")

        lcs_p, bcum_p, bnat_p = _scan_call(g_a[:n_p].reshape(b, t, LANES), g_b[:n_p].reshape(b, t, LANES),
                                           jnp.zeros((b, 1, LANES), F32), lp, "scan_prompt")
        past_lf = jnp.zeros((bs, past, LANES), F32).at[:, :, :fh].set(cache_fox_logf[l])
        lcs_c, _, _ = _scan_call(past_lf, past_lf, jnp.zeros((bs, 1, LANES), F32), pc, "scan_cache")
        lcs_s, bcum_s, bnat_s = _scan_call(g_a[n_p:].reshape(bs, ts, LANES), g_b[n_p:].reshape(bs, ts, LANES),
                                           lcs_c[:, past - 1:past, :], ts, "scan_sample")

        lk_p = (jnp.transpose(lcs_p[:, :, :fh], (0, 2, 1)) * LOG2E).reshape(b * fh, t // tb, 1, tb)
        y_fox = _fox_prompt_call(q_b, k_b, v_b, lk_p, b=b, t=t, fh=fh, dh=dh, tb=tb, nh=nh)
        lk_c = (jnp.transpose(lcs_c[:, :, :fh], (0, 2, 1)) * LOG2E).reshape(bs * fh, 1, past)
        lk_n = (jnp.transpose(lcs_s[:, :, :fh], (0, 2, 1)) * LOG2E).reshape(bs * fh, 1, ts)
        y_fox = _fox_sample_call(y_fox, q_b, k_b, v_b, cache_fox_k[l].reshape(bs, past, fw),
                                 cache_fox_v[l].reshape(bs, past, fw), lcs_s.reshape(n_s, LANES), lk_c, lk_n,
                                 n_prompt=n_p, bs=bs, ts=ts, fh=fh, dh=dh)

        gain_m = ml_out_g[l].reshape(mh, 1, dv)
        br_p = jnp.transpose(bnat_p[:, :, fh:fh + mh], (0, 2, 1)).reshape(b * mh, t // lp, 1, lp)
        y_ml, c1_p, n1_p, m1_p = _mlstm_call(
            m_q, m_k, m_v, m_o, bcum_p.reshape(n_p, LANES), bnat_p.reshape(n_p, LANES), br_p, gain_m,
            zeros_c, zeros_n, zeros_m, row0=0, nb=b, nc=t // lp, chunk=lp, mh=mh, dqk=dqk, dv=dv, hoff=fh,
            y_prev=None, name="mlstm_prompt")
        br_s = jnp.transpose(bnat_s[:, :, fh:fh + mh], (0, 2, 1)).reshape(bs * mh, 1, 1, ts)
        y_ml, c1_s, n1_s, m1_s = _mlstm_call(
            m_q, m_k, m_v, m_o, bcum_s.reshape(n_s, LANES), bnat_s.reshape(n_s, LANES), br_s, gain_m,
            state_mlstm_c[l], state_mlstm_n[l].reshape(bs, mh, 1, dqk),
            jnp.broadcast_to(state_mlstm_m[l][:, :, None, None], (bs, mh, 1, LANES)),
            row0=n_p, nb=bs, nc=1, chunk=ts, mh=mh, dqk=dqk, dv=dv, hoff=fh, y_prev=y_ml, name="mlstm_sample")

        x1, h2, gates = _merge_call(
            x, g1n, sc1, sh1, gt1, y_fox, y_ml, w_gate[l].astype(BF16), b_gate[l].reshape(1, 2 * d),
            w_out[l].astype(BF16), g2n, sc2, sh2, wr_hi, wr_lo, br, tm=tm_b, tc=tc, grp=grp, n_exp=n_exp, epg=epg)
        x = _moe_call(h2, gates, x1, gt2, w_e_gate[l].astype(BF16), w_e_up[l].astype(BF16),
                      w_e_down[l].astype(BF16), tm=tm_b, grp=grp)

        outs_p.append((k_p.reshape(b, t, fh, dh), v_p.reshape(b, t, fh, dh), g_a[:n_p, :fh].reshape(b, t, fh),
                       c1_p, n1_p.reshape(b, mh, dqk), m1_p[:, :, 0, 0]))
        outs_s.append((k_s.reshape(bs, ts, fh, dh), v_s.reshape(bs, ts, fh, dh), g_a[n_p:, :fh].reshape(bs, ts, fh),
                       c1_s, n1_s.reshape(bs, mh, dqk), m1_s[:, :, 0, 0]))

    def stack(outs, i):
        return jnp.stack([o[i] for o in outs], axis=0)

    return (x[:n_p].reshape(b, t, d), x[n_p:].reshape(bs, ts, d),
            stack(outs_p, 0), stack(outs_p, 1), stack(outs_p, 2), stack(outs_p, 3), stack(outs_p, 4), stack(outs_p, 5),
            stack(outs_s, 0), stack(outs_s, 1), stack(outs_s, 2), stack(outs_s, 3), stack(outs_s, 4), stack(outs_s, 5))
```

```python
import functools

import jax
import jax.numpy as jnp
from jax import lax
from jax.experimental import pallas as pl
from jax.experimental.pallas import tpu as pltpu

EPS = 1e-6
LANES = 128
VMEM_LIMIT = 56 * 1024 * 1024
BF16 = jnp.bfloat16
F32 = jnp.float32
NEG_INF = float("-inf")
LOG2E = 1.4426950408889634


def _dot(a, b):
    return jnp.dot(a, b, preferred_element_type=F32)


def _dot_nt(a, b):
    return lax.dot_general(a, b, (((1,), (1,)), ((), ())), preferred_element_type=F32)


def _split3(x):
    hi = x.astype(BF16)
    r = x - hi.astype(F32)
    mid = r.astype(BF16)
    lo = (r - mid.astype(F32)).astype(BF16)
    return hi, mid, lo


def _lane_select(x, idx):
    rows = lax.broadcasted_iota(jnp.int32, (LANES, LANES), 0)
    onehot = jnp.where(rows == idx, 1.0, 0.0).astype(BF16)
    hi, mid, lo = _split3(x)
    return _dot(hi, onehot) + _dot(mid, onehot) + _dot(lo, onehot)


def _log_sigmoid(x):
    return jnp.minimum(x, 0.0) - jnp.log1p(jnp.exp(-jnp.abs(x)))


def _sigmoid(x):
    return 1.0 / (1.0 + jnp.exp(-x))


def _params(*sem, flags=None):
    return pltpu.CompilerParams(dimension_semantics=sem, vmem_limit_bytes=VMEM_LIMIT, flags=flags)


def _pick(n_list, cands):
    for c in cands:
        if all(n % c == 0 for n in n_list):
            return c
    raise ValueError(f"no tile in {cands} divides {n_list}")


def _ada_kernel(c_ref, w_ref, b_ref, o_ref):
    c = c_ref[...]
    sc = (c * _sigmoid(c)).astype(BF16)
    o_ref[...] = _dot(sc, w_ref[...].astype(BF16)) + b_ref[...]


def _ada_call(c_all, w_ada, b_ada):
    depth, d, n6 = w_ada.shape
    r = c_all.shape[0]
    tn = _pick([n6], (1024, 512, 256, 128))
    return pl.pallas_call(
        _ada_kernel,
        grid=(depth, n6 // tn),
        in_specs=[pl.BlockSpec((r, d), lambda l, j: (0, 0)),
                  pl.BlockSpec((None, d, tn), lambda l, j: (l, 0, j)),
                  pl.BlockSpec((None, 1, tn), lambda l, j: (l, 0, j))],
        out_specs=pl.BlockSpec((None, r, tn), lambda l, j: (l, 0, j)),
        out_shape=jax.ShapeDtypeStruct((depth, r, n6), F32),
        compiler_params=_params("arbitrary", "arbitrary"),
        name="ada_mod",
    )(c_all, w_ada, b_ada.reshape(depth, 1, n6))


def _norm_mod(x, g, sc, sh, grp):
    tm, d = x.shape
    ms = jnp.mean(x * x, axis=1, keepdims=True)
    xn = x * lax.rsqrt(ms + EPS) * g
    h3 = xn.reshape(tm // grp, grp, d) * (1.0 + sc) + sh
    return h3.reshape(tm, d)


def _head_norm(z, gain, dh):
    outs = []
    for hh in range(z.shape[1] // dh):
        zz = z[:, hh * dh:(hh + 1) * dh]
        ms = jnp.mean(zz * zz, axis=1, keepdims=True)
        outs.append(zz * lax.rsqrt(ms + EPS) * gain[:, hh * dh:(hh + 1) * dh])
    return outs


_SEG_OUTS = {"hn": 1, "hn_split": 3, "split": 3, "bf16": 1, "sig": 1}


def _proj_kernel(*refs, segs, grp, dh, npt, with_small, n_alias):
    x_ref, g_ref, sc_ref, sh_ref, w_ref, gain_ref = refs[:6]
    pos = 6
    if with_small:
        ws_ref, bs_ref = refs[6:8]
        pos = 8
    pos += n_alias
    h_scr = refs[-1]
    out_refs = refs[pos:-1]
    i = pl.program_id(0)
    j = pl.program_id(1)

    @pl.when(j == 0)
    def _():
        h = _norm_mod(x_ref[...], g_ref[...], sc_ref[...], sh_ref[...], grp).astype(BF16)
        h_scr[...] = h
        if with_small:
            ga_ref, gb_ref = out_refs[-2:]
            zs = _dot(h, ws_ref[...]) + bs_ref[...]
            lane = lax.broadcasted_iota(jnp.int32, (1, LANES), 1)
            ga_ref[...] = jnp.where(lane < with_small, _log_sigmoid(zs[:, :LANES]), 0.0)
            gb_ref[...] = zs[:, LANES:]

    z = _dot(h_scr[...], w_ref[...])
    tn = z.shape[1]
    o = 0
    j0 = 0
    for kind, nt in segs:
        outs = out_refs[o:o + _SEG_OUTS[kind]]
        o += _SEG_OUTS[kind]

        def seg_body(kind=kind, outs=outs):
            if kind in ("hn", "hn_split"):
                parts = _head_norm(z, gain_ref[...], dh)
            elif kind == "sig":
                parts = [_sigmoid(z)]
            else:
                parts = [z]
            w = tn // len(parts)
            if kind in ("hn_split", "split"):
                p_ref, s_ref, b_ref = outs
                for k, part in enumerate(parts):
                    b_ref[:, k * w:(k + 1) * w] = part.astype(BF16)

                @pl.when(i < npt)
                def _():
                    for k, part in enumerate(parts):
                        p_ref[:, k * w:(k + 1) * w] = part

                @pl.when(i >= npt)
                def _():
                    for k, part in enumerate(parts):
                        s_ref[:, k * w:(k + 1) * w] = part
            else:
                for k, part in enumerate(parts):
                    outs[0][:, k * w:(k + 1) * w] = part.astype(BF16)

        pl.when((j >= j0) & (j < j0 + nt))(seg_body)
        j0 += nt


def _proj_call(x, g, sc, sh, w, gain, segs, *, tm, tn, grp, dh, n_prompt, small=None, name,
               layer=0, depth=1, stacks=None):
    n, d = x.shape
    nt_total = w.shape[1] // tn
    npt = n_prompt // tm
    nst = (n - n_prompt) // tm
    gr = tm // grp

    def clipj(j, j0, nt):
        return jnp.clip(j - j0, 0, nt - 1)

    in_specs = [pl.BlockSpec((tm, d), lambda i, j: (i, 0)),
                pl.BlockSpec((1, d), lambda i, j: (0, 0)),
                pl.BlockSpec((gr, 1, d), lambda i, j: (i, 0, 0)),
                pl.BlockSpec((gr, 1, d), lambda i, j: (i, 0, 0)),
                pl.BlockSpec((d, tn), lambda i, j: (0, j)),
                pl.BlockSpec((1, tn), lambda i, j: (0, j))]
    args = [x, g, sc, sh, w, gain]
    n_gate = 0
    if small is not None:
        ws, bs, n_gate = small
        in_specs += [pl.BlockSpec(ws.shape, lambda i, j: (0, 0)),
                     pl.BlockSpec(bs.shape, lambda i, j: (0, 0))]
        args += [ws, bs]

    out_specs, out_shapes, stack_outs = [], [], []
    j0 = 0
    for kind, nt in segs:
        width = nt * tn
        all_spec = pl.BlockSpec((tm, tn), functools.partial(
            lambda i, j, j0, nt: (i, clipj(j, j0, nt)), j0=j0, nt=nt))
        if kind in ("hn_split", "split"):
            stack_outs.append(len(out_specs))
            out_specs.append(pl.BlockSpec((None, tm, tn), functools.partial(
                lambda i, j, j0, nt: (layer, jnp.minimum(i, npt - 1),
                                      jnp.where(i < npt, clipj(j, j0, nt), nt - 1)), j0=j0, nt=nt)))
            out_shapes.append(jax.ShapeDtypeStruct((depth, n_prompt, width), F32))
            stack_outs.append(len(out_specs))
            out_specs.append(pl.BlockSpec((None, tm, tn), functools.partial(
                lambda i, j, j0, nt: (layer, jnp.maximum(i - npt, 0),
                                      jnp.where(i >= npt, clipj(j, j0, nt), 0)), j0=j0, nt=nt)))
            out_shapes.append(jax.ShapeDtypeStruct((depth, nst * tm, width), F32))
        out_specs.append(all_spec)
        out_shapes.append(jax.ShapeDtypeStruct((n, width), BF16))
        j0 += nt
    assert j0 == nt_total
    if small is not None:
        for _ in range(2):
            out_specs.append(pl.BlockSpec((tm, LANES), lambda i, j: (i, 0)))
            out_shapes.append(jax.ShapeDtypeStruct((n, LANES), F32))
    aliases = {}
    if stacks is not None:
        assert len(stacks) == len(stack_outs)
        for arr, o in zip(stacks, stack_outs):
            aliases[len(args)] = o
            in_specs.append(pl.BlockSpec(memory_space=pl.ANY))
            args.append(arr)

    return pl.pallas_call(
        functools.partial(_proj_kernel, segs=tuple(segs), grp=grp, dh=dh, npt=npt, with_small=n_gate,
                          n_alias=len(aliases)),
        grid=(n // tm, nt_total),
        in_specs=in_specs,
        out_specs=out_specs,
        out_shape=out_shapes,
        scratch_shapes=[pltpu.VMEM((tm, d), BF16)],
        input_output_aliases=aliases,
        compiler_params=_params("arbitrary", "arbitrary"),
        name=name,
    )(*args)


def _scan_kernel(ga_ref, gb_ref, init_ref, lcs_ref, bcum_ref, b_ref, carry):
    c = pl.program_id(1)

    @pl.when(c == 0)
    def _():
        carry[...] = init_ref[...]

    a = ga_ref[...]
    ln = a.shape[0]
    r = lax.broadcasted_iota(jnp.int32, (ln, ln), 0)
    cc = lax.broadcasted_iota(jnp.int32, (ln, ln), 1)
    tri = jnp.where(r >= cc, 1.0, 0.0).astype(BF16)
    hi, mid, lo = _split3(a)
    cs = _dot(tri, hi) + _dot(tri, mid) + _dot(tri, lo)
    lcs_ref[...] = cs + carry[...]
    bcum_ref[...] = cs
    b_ref[...] = gb_ref[...] - cs
    carry[...] = carry[...] + cs[ln - 1:ln, :]


def _scan_call(ga, gb, init, chunk, name):
    b, t, _ = ga.shape
    spec = pl.BlockSpec((None, chunk, LANES), lambda bi, c: (bi, c, 0))
    shp = jax.ShapeDtypeStruct((b, t, LANES), F32)
    return pl.pallas_call(
        _scan_kernel,
        grid=(b, t // chunk),
        in_specs=[spec, spec, pl.BlockSpec((None, 1, LANES), lambda bi, c: (bi, 0, 0))],
        out_specs=[spec, spec, spec],
        out_shape=[shp, shp, shp],
        scratch_shapes=[pltpu.VMEM((1, LANES), F32)],
        compiler_params=_params("arbitrary", "arbitrary"),
        name=name,
    )(ga, gb, init)


def _bias_lanes(x_row, own):
    x = jnp.broadcast_to(x_row, (LANES, x_row.shape[1])).T
    hi, mid, lo = _split3(x)
    lane = lax.broadcasted_iota(jnp.int32, x.shape, 1)
    base = 0 if own else 3
    const = jnp.where((lane >= 3 - base) & (lane < 6 - base), -1.0 if own else 1.0, 0.0).astype(BF16)
    return jnp.where(lane == base, hi, jnp.where(lane == base + 1, mid, jnp.where(lane == base + 2, lo, const)))


def _fox_kernel(q_ref, k_ref, v_ref, lrow_ref, o_ref, vt_scr, ka_scr, qa_scr, m_scr, l_scr, acc_scr,
                sa_scr, ca_scr, sb_scr, cb_scr, *, tb, nh, dh):
    qi = pl.program_id(2)
    nblk = k_ref.shape[0] // tb

    @pl.when(qi == 0)
    def _():
        def prep(c, carry):
            off = pl.multiple_of(c * tb, tb)
            for hh in range(nh):
                v = v_ref[pl.ds(off, tb), hh * dh:(hh + 1) * dh]
                vt_scr[hh, c] = v.astype(F32).T.astype(BF16)
                ka_scr[hh, c, :, :dh] = k_ref[pl.ds(off, tb), hh * dh:(hh + 1) * dh]
                ka_scr[hh, c, :, dh:] = _bias_lanes(lrow_ref[hh, c], False)
            return carry

        lax.fori_loop(0, nblk, prep, 0)

    for hh in range(nh):
        m_scr[hh] = jnp.full((1, tb), NEG_INF, F32)
        l_scr[hh] = jnp.zeros((1, tb), F32)
        acc_scr[hh] = jnp.zeros((dh, tb), F32)
        qa_scr[hh, :, :dh] = q_ref[:, hh * dh:(hh + 1) * dh]
        qa_scr[hh, :, dh:] = _bias_lanes(lrow_ref[hh, qi], True)

    def scores(ki, s_buf, c_buf, masked):
        for hh in range(nh):
            st = _dot_nt(ka_scr[hh, ki], qa_scr[hh])
            if masked:
                r = lax.broadcasted_iota(jnp.int32, (tb, tb), 0)
                c = lax.broadcasted_iota(jnp.int32, (tb, tb), 1)
                st = jnp.where(r <= c, st, NEG_INF)
            s_buf[hh] = st
            c_buf[hh] = jnp.max(st, axis=0, keepdims=True)

    def accumulate(ki, s_buf, c_buf):
        for hh in range(nh):
            m_prev = m_scr[hh]
            m_new = jnp.maximum(m_prev, c_buf[hh])
            p = jnp.exp2(s_buf[hh] - m_new)
            alpha = jnp.exp2(m_prev - m_new)
            l_scr[hh] = alpha * l_scr[hh] + jnp.sum(p, axis=0, keepdims=True)
            acc_scr[hh] = alpha * acc_scr[hh] + _dot(vt_scr[hh, ki], p.astype(BF16))
            m_scr[hh] = m_new

    scores(qi, sa_scr, ca_scr, True)

    def pair(tstep, carry):
        k1 = qi - 1 - 2 * tstep
        scores(k1, sb_scr, cb_scr, False)
        accumulate(k1 + 1, sa_scr, ca_scr)
        scores(k1 - 1, sa_scr, ca_scr, False)
        accumulate(k1, sb_scr, cb_scr)
        return carry

    lax.fori_loop(0, qi // 2, pair, 0)

    @pl.when(qi % 2 == 1)
    def _():
        scores(0, sb_scr, cb_scr, False)
        accumulate(1, sa_scr, ca_scr)
        accumulate(0, sb_scr, cb_scr)

    @pl.when(qi % 2 == 0)
    def _():
        accumulate(0, sa_scr, ca_scr)

    for hh in range(nh):
        o = acc_scr[hh] * (1.0 / l_scr[hh])
        o_ref[:, hh * dh:(hh + 1) * dh] = o.T.astype(o_ref.dtype)


def _fox_prompt_call(q, k, v, lrows, *, b, t, fh, dh, tb, nh):
    n = q.shape[0]
    nq = t // tb
    hw = nh * dh
    return pl.pallas_call(
        functools.partial(_fox_kernel, tb=tb, nh=nh, dh=dh),
        grid=(b, fh // nh, nq),
        in_specs=[pl.BlockSpec((tb, hw), lambda bi, h, qi: (bi * nq + qi, h)),
                  pl.BlockSpec((t, hw), lambda bi, h, qi: (bi, h)),
                  pl.BlockSpec((t, hw), lambda bi, h, qi: (bi, h)),
                  pl.BlockSpec((nh, nq, 1, tb), lambda bi, h, qi: (bi * (fh // nh) + h, 0, 0, 0))],
        out_specs=pl.BlockSpec((tb, hw), lambda bi, h, qi: (bi * nq + qi, h)),
        out_shape=jax.ShapeDtypeStruct((n, fh * dh), BF16),
        scratch_shapes=[pltpu.VMEM((nh, nq, dh, tb), BF16), pltpu.VMEM((nh, nq, tb, dh + LANES), BF16),
                        pltpu.VMEM((nh, tb, dh + LANES), BF16),
                        pltpu.VMEM((nh, 1, tb), F32), pltpu.VMEM((nh, 1, tb), F32),
                        pltpu.VMEM((nh, dh, tb), F32),
                        pltpu.VMEM((nh, tb, tb), F32), pltpu.VMEM((nh, 1, tb), F32),
                        pltpu.VMEM((nh, tb, tb), F32), pltpu.VMEM((nh, 1, tb), F32)],
        compiler_params=_params("arbitrary", "arbitrary", "arbitrary"),
        name="fox_prompt",
    )(q, k, v, lrows)


def _fox_sample_kernel(q_ref, kp_ref, vp_ref, kn_ref, vn_ref, lcs_ref, lkp_ref, lkn_ref, y_any, o_ref):
    del y_any
    h = pl.program_id(1)
    ts = q_ref.shape[0]
    q = q_ref[...]
    lq = _lane_select(lcs_ref[...], h)[:, :1] * LOG2E
    sp = _dot_nt(q, kp_ref[...].astype(BF16)) + lq - lkp_ref[...]
    sn = _dot_nt(q, kn_ref[...]) + lq - lkn_ref[...]
    r = lax.broadcasted_iota(jnp.int32, (ts, ts), 0)
    c = lax.broadcasted_iota(jnp.int32, (ts, ts), 1)
    sn = jnp.where(r >= c, sn, NEG_INF)
    m = jnp.maximum(jnp.max(sp, axis=1, keepdims=True), jnp.max(sn, axis=1, keepdims=True))
    pp = jnp.exp2(sp - m)
    pn = jnp.exp2(sn - m)
    den = jnp.sum(pp, axis=1, keepdims=True) + jnp.sum(pn, axis=1, keepdims=True)
    acc = _dot(pp.astype(BF16), vp_ref[...].astype(BF16)) + _dot(pn.astype(BF16), vn_ref[...])
    o_ref[...] = (acc / den).astype(o_ref.dtype)


def _fox_sample_call(y, q, kb, vb, cache_k, cache_v, lcs_s, lkp_rows, lkn_rows, *, n_prompt, bs, ts, fh, dh):
    p = cache_k.shape[1]
    r0 = n_prompt // ts
    return pl.pallas_call(
        _fox_sample_kernel,
        grid=(bs, fh),
        in_specs=[pl.BlockSpec((ts, dh), lambda bi, h: (r0 + bi, h)),
                  pl.BlockSpec((None, p, dh), lambda bi, h: (bi, 0, h)),
                  pl.BlockSpec((None, p, dh), lambda bi, h: (bi, 0, h)),
                  pl.BlockSpec((ts, dh), lambda bi, h: (r0 + bi, h)),
                  pl.BlockSpec((ts, dh), lambda bi, h: (r0 + bi, h)),
                  pl.BlockSpec((ts, LANES), lambda bi, h: (bi, 0)),
                  pl.BlockSpec((None, 1, p), lambda bi, h: (bi * fh + h, 0, 0)),
                  pl.BlockSpec((None, 1, ts), lambda bi, h: (bi * fh + h, 0, 0)),
                  pl.BlockSpec(memory_space=pl.ANY)],
        out_specs=pl.BlockSpec((ts, dh), lambda bi, h: (r0 + bi, h)),
        out_shape=jax.ShapeDtypeStruct(y.shape, y.dtype),
        input_output_aliases={8: 0},
        compiler_params=_params("arbitrary", "arbitrary"),
        name="fox_sample",
    )(q, cache_k, cache_v, kb, vb, lcs_s, lkp_rows, lkn_rows, y)


def _mlstm_kernel(*refs, hoff, aliased):
    if aliased:
        refs = refs[:11] + refs[12:]
    (q_ref, k_ref, v_ref, o_ref, bcum_ref, bnat_ref, brow_ref, g_ref, c0_ref, n0_ref, m0_ref,
     y_ref, c1_ref, n1_ref, m1_ref, c_scr, n_scr, m_scr) = refs
    h = pl.program_id(1)
    c = pl.program_id(2)
    nc = pl.num_programs(2)

    @pl.when(c == 0)
    def _():
        c_scr[...] = c0_ref[...]
        n_scr[...] = n0_ref[...]
        m_scr[...] = m0_ref[...]

    q = q_ref[...]
    k = k_ref[...]
    v = v_ref[...]
    ln = q.shape[0]
    bcum = _lane_select(bcum_ref[...], hoff + h)[:, :1]
    bcol = _lane_select(bnat_ref[...], hoff + h)[:, :1]
    brow = brow_ref[...]
    m0 = m_scr[:, :1]
    r = lax.broadcasted_iota(jnp.int32, (ln, ln), 0)
    cc = lax.broadcasted_iota(jnp.int32, (ln, ln), 1)
    bm = jnp.where(r >= cc, brow, NEG_INF)
    g = jnp.maximum(m0, jnp.max(bm, axis=1, keepdims=True))
    dmat = jnp.exp(bm - g)
    s = _dot_nt(q, k) * dmat
    w_inter = jnp.exp(m0 - g)
    cmat = c_scr[...]
    num = _dot(s.astype(BF16), v) + _dot(q, cmat.astype(BF16)) * w_inter
    qn = jnp.sum(q.astype(F32) * n_scr[...], axis=1, keepdims=True)
    den = jnp.sum(s, axis=1, keepdims=True) + w_inter * qn
    den = jnp.maximum(jnp.abs(den), jnp.exp(-(bcum + g)))
    hval = num * (1.0 / den)
    ms = jnp.mean(hval * hval, axis=1, keepdims=True)
    y = hval * lax.rsqrt(ms + EPS) * g_ref[...] * o_ref[...].astype(F32)
    y_ref[...] = y.astype(y_ref.dtype)

    g_last = g[ln - 1:ln, :]
    w_key = jnp.exp(bcol - g_last)
    f_tot = jnp.exp(m0 - g_last)
    kw = k.astype(F32) * w_key
    c_new = f_tot * cmat + _dot(kw.T.astype(BF16), v)
    n_new = f_tot * n_scr[...] + jnp.sum(kw, axis=0, keepdims=True)
    m_new = jnp.broadcast_to(bcum[ln - 1:ln, :] + g_last, m_scr.shape)
    c_scr[...] = c_new
    n_scr[...] = n_new
    m_scr[...] = m_new

    @pl.when(c == nc - 1)
    def _():
        c1_ref[...] = c_new
        n1_ref[...] = n_new
        m1_ref[...] = m_new


def _mlstm_call(mq, mk, mv, mo, bcum, bnat, brows, gain, c0, n0, m0, *, row0, nb, nc, chunk, mh, dqk, dv,
                hoff, y_prev, name):
    n = mq.shape[0]
    rb0 = row0 // chunk

    def rows(bi, h, c):
        return rb0 + bi * nc + c

    in_specs = [pl.BlockSpec((chunk, dqk), lambda bi, h, c: (rows(bi, h, c), h)),
                pl.BlockSpec((chunk, dqk), lambda bi, h, c: (rows(bi, h, c), h)),
                pl.BlockSpec((chunk, dv), lambda bi, h, c: (rows(bi, h, c), h)),
                pl.BlockSpec((chunk, dv), lambda bi, h, c: (rows(bi, h, c), h)),
                pl.BlockSpec((chunk, LANES), lambda bi, h, c: (bi * nc + c, 0)),
                pl.BlockSpec((chunk, LANES), lambda bi, h, c: (bi * nc + c, 0)),
                pl.BlockSpec((None, None, 1, chunk), lambda bi, h, c: (bi * mh + h, c, 0, 0)),
                pl.BlockSpec((None, 1, dv), lambda bi, h, c: (h, 0, 0)),
                pl.BlockSpec((None, None, dqk, dv), lambda bi, h, c: (bi, h, 0, 0)),
                pl.BlockSpec((None, None, 1, dqk), lambda bi, h, c: (bi, h, 0, 0)),
                pl.BlockSpec((None, None, 1, LANES), lambda bi, h, c: (bi, h, 0, 0))]
    args = [mq, mk, mv, mo, bcum, bnat, brows, gain, c0, n0, m0]
    aliases = {}
    if y_prev is not None:
        in_specs.append(pl.BlockSpec(memory_space=pl.ANY))
        args.append(y_prev)
        aliases = {len(args) - 1: 0}
    return pl.pallas_call(
        functools.partial(_mlstm_kernel, hoff=hoff, aliased=y_prev is not None),
        grid=(nb, mh, nc),
        in_specs=in_specs,
        out_specs=[pl.BlockSpec((chunk, dv), lambda bi, h, c: (rows(bi, h, c), h)),
                   pl.BlockSpec((None, None, dqk, dv), lambda bi, h, c: (bi, h, 0, 0)),
                   pl.BlockSpec((None, None, 1, dqk), lambda bi, h, c: (bi, h, 0, 0)),
                   pl.BlockSpec((None, None, 1, LANES), lambda bi, h, c: (bi, h, 0, 0))],
        out_shape=[jax.ShapeDtypeStruct((n, mh * dv), BF16),
                   jax.ShapeDtypeStruct((nb, mh, dqk, dv), F32),
                   jax.ShapeDtypeStruct((nb, mh, 1, dqk), F32),
                   jax.ShapeDtypeStruct((nb, mh, 1, LANES), F32)],
        scratch_shapes=[pltpu.VMEM((dqk, dv), F32), pltpu.VMEM((1, dqk), F32), pltpu.VMEM((1, LANES), F32)],
        input_output_aliases=aliases,
        compiler_params=_params("arbitrary", "arbitrary", "arbitrary"),
        name=name,
    )(*args)


def _route(logits, b_router, n_exp, epg):
    n_grp = n_exp // epg
    lane = lax.broadcasted_iota(jnp.int32, logits.shape, 1)
    valid = lane < n_exp
    pos = lane % epg
    grp = lane // epg
    s = _sigmoid(logits)
    sb = jnp.where(valid, s + b_router, NEG_INF)

    def nbr(x, d):
        return pltpu.roll(x, (-d) % LANES, axis=1)

    def count_beaten(x, idx, span, step):
        cnt = jnp.zeros(x.shape, F32)
        for d in range(-(span - 1), span):
            if d == 0:
                continue
            other = nbr(x, d * step)
            inside = (idx + d >= 0) & (idx + d < span)
            ahead = (other > x) | ((other == x) & (d < 0))
            cnt = cnt + jnp.where(inside & ahead, 1.0, 0.0)
        return cnt

    top2 = count_beaten(sb, pos, epg, 1) < 2.0
    t2v = jnp.where(top2, sb, 0.0)
    t2v = jnp.where(valid, t2v, NEG_INF)
    gscore = t2v
    for d in range(-(epg - 1), epg):
        if d == 0:
            continue
        inside = (pos + d >= 0) & (pos + d < epg)
        gscore = gscore + jnp.where(inside, nbr(t2v, d), 0.0)
    best_grp = count_beaten(gscore, grp, n_grp, epg) < 1.0
    sel = valid & best_grp & top2
    w = jnp.where(sel, s, 0.0)
    return w / jnp.sum(w, axis=1, keepdims=True), jnp.where(sel, 1.0, 0.0)


def _merge_kernel(x_ref, g1n_ref, sc1_ref, sh1_ref, gt1_ref, yf_ref, ym_ref, wga_ref, wgb_ref, bga_ref, bgb_ref,
                  wo_ref, g2n_ref, sc2_ref, sh2_ref, wrh_ref, wrl_ref, br_ref,
                  x1_ref, h2_ref, gates_ref, sel_ref, h_scr, acc_scr, *, grp, n_exp, epg):
    j = pl.program_id(1)
    nj = pl.num_programs(1)

    @pl.when(j == 0)
    def _():
        h_scr[...] = _norm_mod(x_ref[...], g1n_ref[...], sc1_ref[...], sh1_ref[...], grp).astype(BF16)
        acc_scr[...] = jnp.zeros(acc_scr.shape, F32)

    h = h_scr[...]
    ga = _dot(h, wga_ref[...]) + bga_ref[...]
    gb = _dot(h, wgb_ref[...]) + bgb_ref[...]
    mix = _sigmoid(ga) * yf_ref[...].astype(F32) + _sigmoid(gb) * ym_ref[...].astype(F32)
    acc_scr[...] += _dot(mix.astype(BF16), wo_ref[...])

    @pl.when(j == nj - 1)
    def _():
        tm, d = x_ref.shape
        acc3 = acc_scr[...].reshape(tm // grp, grp, d) * gt1_ref[...]
        x1 = x_ref[...] + acc3.reshape(tm, d)
        x1_ref[...] = x1
        h2 = _norm_mod(x1, g2n_ref[...], sc2_ref[...], sh2_ref[...], grp)
        hi = h2.astype(BF16)
        lo = (h2 - hi.astype(F32)).astype(BF16)
        h2_ref[...] = h2
        logits = _dot(hi, wrh_ref[...]) + _dot(lo, wrh_ref[...]) + _dot(hi, wrl_ref[...])
        gates_ref[...], sel_ref[...] = _route(logits, br_ref[...], n_exp, epg)


def _merge_call(x, g1n, sc1, sh1, gt1, yf, ym, wgate, bgate, wout, g2n, sc2, sh2, wr_hi, wr_lo, br,
                *, tm, tc, grp, n_exp, epg):
    n, d = x.shape
    nj = d // tc
    gr = tm // grp
    row = pl.BlockSpec((tm, d), lambda i, j: (i, 0))
    vec = pl.BlockSpec((1, d), lambda i, j: (0, 0))
    mod = pl.BlockSpec((gr, 1, d), lambda i, j: (i, 0, 0))
    return pl.pallas_call(
        functools.partial(_merge_kernel, grp=grp, n_exp=n_exp, epg=epg),
        grid=(n // tm, nj),
        in_specs=[row, vec, mod, mod, mod,
                  pl.BlockSpec((tm, tc), lambda i, j: (i, j)),
                  pl.BlockSpec((tm, tc), lambda i, j: (i, j)),
                  pl.BlockSpec((d, tc), lambda i, j: (0, j)),
                  pl.BlockSpec((d, tc), lambda i, j: (0, nj + j)),
                  pl.BlockSpec((1, tc), lambda i, j: (0, j)),
                  pl.BlockSpec((1, tc), lambda i, j: (0, nj + j)),
                  pl.BlockSpec((tc, d), lambda i, j: (j, 0)),
                  vec, mod, mod,
                  pl.BlockSpec((d, LANES), lambda i, j: (0, 0)),
                  pl.BlockSpec((d, LANES), lambda i, j: (0, 0)),
                  pl.BlockSpec((1, LANES), lambda i, j: (0, 0))],
        out_specs=[row, row, pl.BlockSpec((tm, LANES), lambda i, j: (i, 0)),
                   pl.BlockSpec((tm, LANES), lambda i, j: (i, 0))],
        out_shape=[jax.ShapeDtypeStruct((n, d), F32), jax.ShapeDtypeStruct((n, d), F32),
                   jax.ShapeDtypeStruct((n, LANES), F32), jax.ShapeDtypeStruct((n, LANES), F32)],
        scratch_shapes=[pltpu.VMEM((tm, d), BF16), pltpu.VMEM((tm, d), F32)],
        compiler_params=_params("arbitrary", "arbitrary"),
        name="merge_out",
    )(x, g1n, sc1, sh1, gt1, yf, ym, wgate, wgate, bgate, bgate, wout, g2n, sc2, sh2, wr_hi, wr_lo, br)


def _slot_table_kernel(p0_ref, p1_ref, lo_ref, hi_ref, idx_ref, *, n_tok):
    for k in range(lo_ref.shape[0]):
        def clear(r, c):
            idx_ref[r] = 0
            return c

        lax.fori_loop(lo_ref[k], hi_ref[k], clear, 0)

    def fill(r, c):
        idx_ref[p0_ref[r]] = r
        idx_ref[p1_ref[r]] = r
        return c

    lax.fori_loop(0, n_tok, fill, 0, unroll=8)


def _slot_table_call(pos0, pos1, pad_lo, pad_hi, n_slot):
    n_tok = pos0.shape[0]
    smem = pl.BlockSpec(memory_space=pltpu.SMEM)
    return pl.pallas_call(
        functools.partial(_slot_table_kernel, n_tok=n_tok),
        in_specs=[smem, smem, smem, smem],
        out_specs=smem,
        out_shape=jax.ShapeDtypeStruct((n_slot,), jnp.int32),
        name="moe_slots",
    )(pos0, pos1, pad_lo, pad_hi)


def _gather_rows(src_any, idx_ref, base, dst, sem, n_rows, inline=False):
    def issue(r, c):
        pltpu.make_async_copy(src_any.at[pl.ds(idx_ref[base + r], 1)], dst.at[pl.ds(r, 1)], sem).start()
        return c

    if inline:
        for r in range(n_rows):
            issue(r, 0)
    else:
        lax.fori_loop(0, n_rows, issue, 0, unroll=8)


def _wait_rows(src_any, dst, sem):
    pltpu.make_async_copy(src_any.at[pl.ds(0, dst.shape[0])], dst, sem).wait()


def _expert_kernel(idx_ref, te_ref, nu_ref, h_any, wg_ref, wu_ref, wd_ref, ys_ref,
                   xbuf, wgb, wub, wdb, sem, *, tmx):
    t = pl.program_id(0)
    nt = pl.num_programs(0)
    slot = t % 2

    @pl.when(t == 0)
    def _():
        _gather_rows(h_any, idx_ref, 0, xbuf.at[0], sem.at[0], tmx)

    @pl.when((t == 0) | (te_ref[t] != te_ref[jnp.maximum(t - 1, 0)]))
    def _():
        wgb[...] = wg_ref[...].astype(BF16)
        wub[...] = wu_ref[...].astype(BF16)
        wdb[...] = wd_ref[...].astype(BF16)

    _wait_rows(h_any, xbuf.at[slot], sem.at[slot])
    nxt = jnp.minimum(t + 1, nt - 1) * tmx

    @pl.when(t < nu_ref[0])
    def _():
        x = xbuf[slot].astype(BF16)
        a = _dot(x, wgb[...])
        u = _dot(x, wub[...])
        y = (a * _sigmoid(a)) * u
        ys_ref[...] = _dot(y.astype(BF16), wdb[...])
        _gather_rows(h_any, idx_ref, nxt, xbuf.at[1 - slot], sem.at[1 - slot], tmx, inline=True)

    @pl.when(t >= nu_ref[0])
    def _():
        ys_ref[...] = jnp.zeros(ys_ref.shape, F32)
        _gather_rows(h_any, idx_ref, nxt, xbuf.at[1 - slot], sem.at[1 - slot], tmx)

    @pl.when(t == nt - 1)
    def _():
        _wait_rows(h_any, xbuf.at[1 - slot], sem.at[1 - slot])


def _expert_call(idx, tile_expert, n_used, h2, wg, wu, wd, *, tmx):
    n, d = h2.shape
    _, _, de = wg.shape
    nt = tile_expert.shape[0]
    grid_spec = pltpu.PrefetchScalarGridSpec(
        num_scalar_prefetch=3,
        grid=(nt,),
        in_specs=[pl.BlockSpec(memory_space=pl.ANY),
                  pl.BlockSpec((None, d, de), lambda t, idx, te, nu: (te[t], 0, 0)),
                  pl.BlockSpec((None, d, de), lambda t, idx, te, nu: (te[t], 0, 0)),
                  pl.BlockSpec((None, de, d), lambda t, idx, te, nu: (te[t], 0, 0))],
        out_specs=pl.BlockSpec((tmx, d), lambda t, idx, te, nu: (t, 0)),
        scratch_shapes=[pltpu.VMEM((2, tmx, d), F32), pltpu.VMEM((d, de), BF16), pltpu.VMEM((d, de), BF16),
                        pltpu.VMEM((de, d), BF16), pltpu.SemaphoreType.DMA((2,))])
    return pl.pallas_call(
        functools.partial(_expert_kernel, tmx=tmx),
        grid_spec=grid_spec,
        out_shape=jax.ShapeDtypeStruct((nt * tmx, d), F32),
        compiler_params=_params("arbitrary"),
        name="moe_experts",
    )(idx, tile_expert, n_used, h2, wg, wu, wd)


def _combine_kernel(p0_ref, p1_ref, ys_any, x1_ref, gt2_ref, wp_ref, *refs, tmc, grp, npt):
    out_refs, (buf, sem) = refs[:-2], refs[-2:]
    i = pl.program_id(0)
    nt = pl.num_programs(0)
    slot = i % 2

    def gather(base, s, inline):
        _gather_rows(ys_any, p0_ref, base, buf.at[s, 0], sem.at[s], tmc, inline)
        _gather_rows(ys_any, p1_ref, base, buf.at[s, 1], sem.at[s], tmc, inline)

    def wait(s):
        _wait_rows(ys_any, buf.at[s, 0], sem.at[s])
        _wait_rows(ys_any, buf.at[s, 1], sem.at[s])

    @pl.when(i == 0)
    def _():
        gather(0, 0, False)

    wait(slot)
    gather(jnp.minimum(i + 1, nt - 1) * tmc, 1 - slot, True)
    w = wp_ref[...]
    moe = w[:, 0:1] * buf[slot, 0] + w[:, 1:2] * buf[slot, 1]
    tm, d = moe.shape
    out = x1_ref[...] + (moe.reshape(tm // grp, grp, d) * gt2_ref[...]).reshape(tm, d)
    if len(out_refs) == 1:
        out_refs[0][...] = out
    else:
        @pl.when(i < npt)
        def _():
            out_refs[0][...] = out

        @pl.when(i >= npt)
        def _():
            out_refs[1][...] = out

    @pl.when(i == nt - 1)
    def _():
        wait(1 - slot)


def _combine_call(pos0, pos1, ys, x1, gt2, wpair, *, tmc, grp, n_prompt, split):
    n, d = x1.shape
    gr = tmc // grp
    npt = n_prompt // tmc
    row = lambda i, p0, p1: (i, 0)
    if split:
        out_specs = [pl.BlockSpec((tmc, d), lambda i, p0, p1: (jnp.minimum(i, npt - 1), 0)),
                     pl.BlockSpec((tmc, d), lambda i, p0, p1: (jnp.maximum(i - npt, 0), 0))]
        out_shape = [jax.ShapeDtypeStruct((n_prompt, d), F32), jax.ShapeDtypeStruct((n - n_prompt, d), F32)]
    else:
        out_specs = [pl.BlockSpec((tmc, d), row)]
        out_shape = [jax.ShapeDtypeStruct((n, d), F32)]
    grid_spec = pltpu.PrefetchScalarGridSpec(
        num_scalar_prefetch=2,
        grid=(n // tmc,),
        in_specs=[pl.BlockSpec(memory_space=pl.ANY),
                  pl.BlockSpec((tmc, d), row),
                  pl.BlockSpec((gr, 1, d), lambda i, p0, p1: (i, 0, 0)),
                  pl.BlockSpec((tmc, LANES), row)],
        out_specs=out_specs,
        scratch_shapes=[pltpu.VMEM((2, 2, tmc, d), F32), pltpu.SemaphoreType.DMA((2,))])
    return pl.pallas_call(
        functools.partial(_combine_kernel, tmc=tmc, grp=grp, npt=npt),
        grid_spec=grid_spec,
        out_shape=out_shape,
        compiler_params=_params("arbitrary"),
        name="moe_combine",
    )(pos0, pos1, ys, x1, gt2, wpair)


def _moe_sparse(h2, gates, sel, x1, gt2, wg, wu, wd, *, tmx, tmc, grp, n_prompt, split):
    n, d = x1.shape
    n_exp = wg.shape[0]
    nt = -(-(2 * n + n_exp * (tmx - 1)) // tmx)
    n_slot = nt * tmx
    cs = _scan_call(sel.reshape(1, n, LANES), sel.reshape(1, n, LANES), jnp.zeros((1, 1, LANES), F32),
                    _pick([n], (256, 128, 64)), "scan_route")[0].reshape(n, LANES)
    counts = cs[n - 1, :n_exp].astype(jnp.int32)
    padded = (counts + tmx - 1) // tmx * tmx
    ends = jnp.cumsum(padded)
    off = jnp.zeros((LANES,), jnp.int32).at[:n_exp].set(ends - padded)
    dest = off[None, :] + cs.astype(jnp.int32) - 1
    chosen = sel > 0.0
    pos0 = jnp.min(jnp.where(chosen, dest, n_slot), axis=1)
    pos1 = jnp.max(jnp.where(chosen, dest, -1), axis=1)
    w0 = jnp.sum(jnp.where(chosen & (dest == pos0[:, None]), gates, 0.0), axis=1)
    w1 = jnp.sum(jnp.where(chosen & (dest == pos1[:, None]), gates, 0.0), axis=1)
    wpair = jnp.zeros((n, LANES), F32).at[:, 0].set(w0).at[:, 1].set(w1)
    tile_expert = jnp.minimum(jnp.sum(jnp.arange(nt)[:, None] * tmx >= ends[None, :], axis=1), n_exp - 1)
    n_used = (ends[n_exp - 1] // tmx).reshape(1).astype(jnp.int32)
    pad_lo = jnp.concatenate([ends - padded + counts, ends[n_exp - 1:]]).astype(jnp.int32)
    pad_hi = jnp.concatenate([ends, jnp.full((1,), n_slot, jnp.int32)]).astype(jnp.int32)
    idx = _slot_table_call(pos0, pos1, pad_lo, pad_hi, n_slot)
    ys = _expert_call(idx, tile_expert.astype(jnp.int32), n_used, h2, wg, wu, wd, tmx=tmx)
    return _combine_call(pos0, pos1, ys, x1, gt2, wpair, tmc=tmc, grp=grp, n_prompt=n_prompt, split=split)


def kernel(x_prompt, x_sample, cache_fox_k, cache_fox_v, cache_fox_logf, state_mlstm_c, state_mlstm_n, state_mlstm_m, c_prompt, c_sample, norm1_g, norm2_g, w_ada, b_ada, w_in, fox_qn_g, fox_kn_g, fox_fb, ml_ib, ml_fb, ml_out_g, w_gate, b_gate, w_out, w_router, b_router, w_e_gate, w_e_up, w_e_down):
    b, t, d = x_prompt.shape
    bs, ts, _ = x_sample.shape
    depth = w_in.shape[0]
    past = cache_fox_k.shape[2]
    fh = fox_fb.shape[-1]
    dh = d // fh
    fw = fh * dh
    mh = ml_ib.shape[-1]
    dqk, dv = state_mlstm_c.shape[-2:]
    qw, vw = mh * dqk, mh * dv
    n_exp = w_router.shape[-1]
    epg = n_exp // 4
    n_p, n_s = b * t, bs * ts
    n = n_p + n_s
    grp = ts
    tm_a = _pick([n_p, n_s], (1024, 512, 256, 128))
    tm_b = _pick([n_p, n_s], (512, 256, 128))
    tn = _pick([fw, qw, vw], (512, 256, 128))
    tc = _pick([d], (512, 256, 128))
    tb = _pick([t], (512, 256, 128))
    nh = 2 if fh % 2 == 0 else 1
    tmx = 256
    tmc = _pick([n_p, n_s], (256, 128))
    lp = _pick([t], (256, 128, 64))
    pc = _pick([past], (256, 128, 64))
    assert fh + mh <= LANES and n_exp <= LANES and t % grp == 0 and tm_b % grp == 0

    x = jnp.concatenate([x_prompt.reshape(n_p, d), x_sample.reshape(n_s, d)], axis=0)
    n_seq = b + bs
    r_pad = -(-n_seq // 8) * 8
    c_all = jnp.concatenate([c_prompt, c_sample, jnp.zeros((r_pad - n_seq, d), F32)], axis=0)
    mod = _ada_call(c_all, w_ada, b_ada)
    seq_of_group = jnp.concatenate([jnp.repeat(jnp.arange(b), t // grp), b + jnp.arange(bs)])

    cuts = [0]
    for wdt in (fw, fw, fw, fh, qw, qw, vw, mh, mh, vw):
        cuts.append(cuts[-1] + wdt)
    (c_fq, c_fk, c_fv, c_ff, c_mq, c_mk, c_mv, c_mi, c_mf, c_mo, _) = cuts

    wr_hi = jnp.zeros((d, LANES), F32).at[:, :n_exp].set(w_router)
    wr_lo = (wr_hi - wr_hi.astype(BF16).astype(F32)).astype(BF16)
    wr_hi = wr_hi.astype(BF16)
    br = jnp.zeros((1, LANES), F32).at[0, :n_exp].set(b_router)
    zeros_c = jnp.zeros((b, mh, dqk, dv), F32)
    zeros_n = jnp.zeros((b, mh, 1, dqk), F32)
    zeros_m = jnp.zeros((b, mh, 1, LANES), F32)

    outs_p, outs_s = [], []
    kv_stacks = None
    for l in range(depth):
        modl = mod[l][seq_of_group]
        sh1, sc1, gt1, sh2, sc2, gt2 = [modl[:, k * d:(k + 1) * d].reshape(n // grp, 1, d) for k in range(6)]
        wl = w_in[l]
        w1 = wl[:, :c_ff].astype(BF16)
        w2 = jnp.concatenate([wl[:, c_mq:c_mk] * (dqk ** -0.5), wl[:, c_mk:c_mi], wl[:, c_mo:]], axis=1).astype(BF16)
        ws = jnp.zeros((d, 2 * LANES), F32)
        ws = ws.at[:, :fh].set(wl[:, c_ff:c_mq]).at[:, fh:fh + mh].set(wl[:, c_mf:c_mo])
        ws = ws.at[:, LANES + fh:LANES + fh + mh].set(wl[:, c_mi:c_mf]).astype(BF16)
        bsm = jnp.zeros((1, 2 * LANES), F32)
        bsm = bsm.at[0, :fh].set(fox_fb[l]).at[0, fh:fh + mh].set(ml_fb[l])
        bsm = bsm.at[0, LANES + fh:LANES + fh + mh].set(ml_ib[l])
        gain1 = jnp.concatenate([jnp.tile(fox_qn_g[l] * (LOG2E * dh ** -0.5), fh), jnp.tile(fox_kn_g[l], fh),
                                 jnp.ones((fw,), F32)]).reshape(1, 3 * fw)
        g1n = norm1_g[l].reshape(1, d)
        g2n = norm2_g[l].reshape(1, d)

        q_b, k_p, k_s, k_b, v_p, v_s, v_b, g_a, g_b = _proj_call(
            x, g1n, sc1, sh1, w1, gain1, [("hn", fw // tn), ("hn_split", fw // tn), ("split", fw // tn)],
            tm=tm_a, tn=tn, grp=grp, dh=dh, n_prompt=n_p, small=(ws, bsm, fh + mh), name="proj_fox",
            layer=l, depth=depth, stacks=kv_stacks)
        kv_stacks = [k_p, k_s, v_p, v_s]
        m_q, m_k, m_v, m_o = _proj_call(
            x, g1n, sc1, sh1, w2, jnp.ones((1, w2.shape[1]), F32),
            [("bf16", qw // tn), ("bf16", qw // tn), ("bf16", vw // tn), ("sig", vw // tn)],
            tm=tm_a, tn=tn, grp=grp, dh=dh, n_prompt=n_p, name="proj_mlstm")

        lcs_p, bcum_p, bnat_p = _scan_call(g_a[:n_p].reshape(b, t, LANES), g_b[:n_p].reshape(b, t, LANES),
                                           jnp.zeros((b, 1, LANES), F32), lp, "scan_prompt")
        past_lf = jnp.zeros((bs, past, LANES), F32).at[:, :, :fh].set(cache_fox_logf[l])
        lcs_c, _, _ = _scan_call(past_lf, past_lf, jnp.zeros((bs, 1, LANES), F32), pc, "scan_cache")
        lcs_s, bcum_s, bnat_s = _scan_call(g_a[n_p:].reshape(bs, ts, LANES), g_b[n_p:].reshape(bs, ts, LANES),
                                           lcs_c[:, past - 1:past, :], ts, "scan_sample")

        lk_p = (jnp.transpose(lcs_p[:, :, :fh], (0, 2, 1)) * LOG2E).reshape(b * fh, t // tb, 1, tb)
        y_fox = _fox_prompt_call(q_b, k_b, v_b, lk_p, b=b, t=t, fh=fh, dh=dh, tb=tb, nh=nh)
        lk_c = (jnp.transpose(lcs_c[:, :, :fh], (0, 2, 1)) * LOG2E).reshape(bs * fh, 1, past)
        lk_n = (jnp.transpose(lcs_s[:, :, :fh], (0, 2, 1)) * LOG2E).reshape(bs * fh, 1, ts)
        y_fox = _fox_sample_call(y_fox, q_b, k_b, v_b, cache_fox_k[l].reshape(bs, past, fw),
                                 cache_fox_v[l].reshape(bs, past, fw), lcs_s.reshape(n_s, LANES), lk_c, lk_n,
                                 n_prompt=n_p, bs=bs, ts=ts, fh=fh, dh=dh)

        gain_m = ml_out_g[l].reshape(mh, 1, dv)
        br_p = jnp.transpose(bnat_p[:, :, fh:fh + mh], (0, 2, 1)).reshape(b * mh, t // lp, 1, lp)
        y_ml, c1_p, n1_p, m1_p = _mlstm_call(
            m_q, m_k, m_v, m_o, bcum_p.reshape(n_p, LANES), bnat_p.reshape(n_p, LANES), br_p, gain_m,
            zeros_c, zeros_n, zeros_m, row0=0, nb=b, nc=t // lp, chunk=lp, mh=mh, dqk=dqk, dv=dv, hoff=fh,
            y_prev=None, name="mlstm_prompt")
        br_s = jnp.transpose(bnat_s[:, :, fh:fh + mh], (0, 2, 1)).reshape(bs * mh, 1, 1, ts)
        y_ml, c1_s, n1_s, m1_s = _mlstm_call(
            m_q, m_k, m_v, m_o, bcum_s.reshape(n_s, LANES), bnat_s.reshape(n_s, LANES), br_s, gain_m,
            state_mlstm_c[l], state_mlstm_n[l].reshape(bs, mh, 1, dqk),
            jnp.broadcast_to(state_mlstm_m[l][:, :, None, None], (bs, mh, 1, LANES)),
            row0=n_p, nb=bs, nc=1, chunk=ts, mh=mh, dqk=dqk, dv=dv, hoff=fh, y_prev=y_ml, name="mlstm_sample")

        x1, h2, gates, sel = _merge_call(
            x, g1n, sc1, sh1, gt1, y_fox, y_ml, w_gate[l].astype(BF16), b_gate[l].reshape(1, 2 * d),
            w_out[l].astype(BF16), g2n, sc2, sh2, wr_hi, wr_lo, br, tm=tm_b, tc=tc, grp=grp, n_exp=n_exp, epg=epg)
        moe_out = _moe_sparse(h2, gates, sel, x1, gt2, w_e_gate[l], w_e_up[l], w_e_down[l], tmx=tmx, tmc=tmc,
                              grp=grp, n_prompt=n_p, split=l == depth - 1)
        x = moe_out[0]

        outs_p.append((g_a[:n_p, :fh].reshape(b, t, fh), c1_p, n1_p.reshape(b, mh, dqk), m1_p[:, :, 0, 0]))
        outs_s.append((g_a[n_p:, :fh].reshape(bs, ts, fh), c1_s, n1_s.reshape(bs, mh, dqk), m1_s[:, :, 0, 0]))

    def stack(outs, i):
        return jnp.stack([o[i] for o in outs], axis=0)

    k_p, k_s, v_p, v_s = kv_stacks
    return (moe_out[0].reshape(b, t, d), moe_out[1].reshape(bs, ts, d),
            k_p.reshape(depth, b, t, fh, dh), v_p.reshape(depth, b, t, fh, dh),
            stack(outs_p, 0), stack(outs_p, 1), stack(outs_p, 2), stack(outs_p, 3),
            k_s.reshape(depth, bs, ts, fh, dh), v_s.reshape(depth, bs, ts, fh, dh),
            stack(outs_s, 0), stack(outs_s, 1), stack(outs_s, 2), stack(outs_s, 3))
```

```python
import functools

import jax
import jax.numpy as jnp
from jax import lax
from jax.experimental import pallas as pl
from jax.experimental.pallas import tpu as pltpu

EPS = 1e-6
LANES = 128
VMEM_LIMIT = 56 * 1024 * 1024
BF16 = jnp.bfloat16
F32 = jnp.float32
NEG_INF = float("-inf")
LOG2E = 1.4426950408889634
GATHER_SLOTS = 3
PROJ_ROW_BATCH = 256


def _dot(a, b):
    return jnp.dot(a, b, preferred_element_type=F32)


def _dot_nt(a, b):
    return lax.dot_general(a, b, (((1,), (1,)), ((), ())), preferred_element_type=F32)


def _split3(x):
    hi = x.astype(BF16)
    r = x - hi.astype(F32)
    mid = r.astype(BF16)
    lo = (r - mid.astype(F32)).astype(BF16)
    return hi, mid, lo


def _lane_select(x, idx):
    rows = lax.broadcasted_iota(jnp.int32, (LANES, LANES), 0)
    onehot = jnp.where(rows == idx, 1.0, 0.0).astype(BF16)
    hi, mid, lo = _split3(x)
    return _dot(hi, onehot) + _dot(mid, onehot) + _dot(lo, onehot)


def _log_sigmoid(x):
    return jnp.minimum(x, 0.0) - jnp.log1p(jnp.exp(-jnp.abs(x)))


def _sigmoid(x):
    return 1.0 / (1.0 + jnp.exp(-x))


def _params(*sem, flags=None):
    return pltpu.CompilerParams(dimension_semantics=sem, vmem_limit_bytes=VMEM_LIMIT, flags=flags)


def _pick(n_list, cands):
    for c in cands:
        if all(n % c == 0 for n in n_list):
            return c
    raise ValueError(f"no tile in {cands} divides {n_list}")


def _ada_kernel(c_ref, w_ref, b_ref, o_ref):
    c = c_ref[...]
    sc = (c * _sigmoid(c)).astype(BF16)
    o_ref[...] = _dot(sc, w_ref[...].astype(BF16)) + b_ref[...]


def _ada_call(c_all, w_ada, b_ada):
    depth, d, n6 = w_ada.shape
    r = c_all.shape[0]
    tn = _pick([n6], (1024, 512, 256, 128))
    return pl.pallas_call(
        _ada_kernel,
        grid=(depth, n6 // tn),
        in_specs=[pl.BlockSpec((r, d), lambda l, j: (0, 0)),
                  pl.BlockSpec((None, d, tn), lambda l, j: (l, 0, j)),
                  pl.BlockSpec((None, 1, tn), lambda l, j: (l, 0, j))],
        out_specs=pl.BlockSpec((None, r, tn), lambda l, j: (l, 0, j)),
        out_shape=jax.ShapeDtypeStruct((depth, r, n6), F32),
        compiler_params=_params("arbitrary", "arbitrary"),
        name="ada_mod",
    )(c_all, w_ada, b_ada.reshape(depth, 1, n6))


def _norm_mod(x, g, sc, sh, grp):
    tm, d = x.shape
    ms = jnp.mean(x * x, axis=1, keepdims=True)
    xn = x * lax.rsqrt(ms + EPS) * g
    h3 = xn.reshape(tm // grp, grp, d) * (1.0 + sc) + sh
    return h3.reshape(tm, d)


def _head_norm(z, gain, dh):
    outs = []
    for hh in range(z.shape[1] // dh):
        zz = z[:, hh * dh:(hh + 1) * dh]
        ms = jnp.mean(zz * zz, axis=1, keepdims=True)
        outs.append(zz * lax.rsqrt(ms + EPS) * gain[:, hh * dh:(hh + 1) * dh])
    return outs


_SEG_OUTS = {"hn": 1, "hn_split": 3, "split": 3, "bf16": 1, "sig": 1}


def _proj_kernel(*refs, segs, grp, dh, npt, with_small, n_alias):
    x_ref, g_ref, sc_ref, sh_ref, w_ref, gain_ref = refs[:6]
    pos = 6
    if with_small:
        ws_ref, bs_ref = refs[6:8]
        pos = 8
    pos += n_alias
    h_scr = refs[-1]
    out_refs = refs[pos:-1]
    i = pl.program_id(0)
    j = pl.program_id(1)

    @pl.when(j == 0)
    def _():
        h = _norm_mod(x_ref[...], g_ref[...], sc_ref[...], sh_ref[...], grp).astype(BF16)
        h_scr[...] = h
        if with_small:
            ga_ref, gb_ref = out_refs[-2:]
            zs = _dot(h, ws_ref[...]) + bs_ref[...]
            lane = lax.broadcasted_iota(jnp.int32, (1, LANES), 1)
            ga_ref[...] = jnp.where(lane < with_small, _log_sigmoid(zs[:, :LANES]), 0.0)
            gb_ref[...] = zs[:, LANES:]

    tm = h_scr.shape[0]
    tn = w_ref.shape[1]
    mb = min(tm, PROJ_ROW_BATCH)
    o = 0
    j0 = 0
    for kind, nt in segs:
        outs = out_refs[o:o + _SEG_OUTS[kind]]
        o += _SEG_OUTS[kind]

        def seg_rows(rows, kind, outs):
            z = _dot(h_scr[rows, :], w_ref[...])
            if kind in ("hn", "hn_split"):
                parts = _head_norm(z, gain_ref[...], dh)
            elif kind == "sig":
                parts = [_sigmoid(z)]
            else:
                parts = [z]
            w = tn // len(parts)
            if kind in ("hn_split", "split"):
                p_ref, s_ref, b_ref = outs
                for k, part in enumerate(parts):
                    b_ref[rows, k * w:(k + 1) * w] = part.astype(BF16)

                @pl.when(i < npt)
                def _():
                    for k, part in enumerate(parts):
                        p_ref[rows, k * w:(k + 1) * w] = part

                @pl.when(i >= npt)
                def _():
                    for k, part in enumerate(parts):
                        s_ref[rows, k * w:(k + 1) * w] = part
            else:
                for k, part in enumerate(parts):
                    outs[0][rows, k * w:(k + 1) * w] = part.astype(BF16)

        def seg_body(kind=kind, outs=outs):
            for rb in range(tm // mb):
                seg_rows(slice(rb * mb, (rb + 1) * mb), kind, outs)

        pl.when((j >= j0) & (j < j0 + nt))(seg_body)
        j0 += nt


def _proj_call(x, g, sc, sh, w, gain, segs, *, tm, tn, grp, dh, n_prompt, small=None, name,
               layer=0, depth=1, stacks=None):
    n, d = x.shape
    nt_total = w.shape[1] // tn
    npt = n_prompt // tm
    nst = (n - n_prompt) // tm
    gr = tm // grp

    def clipj(j, j0, nt):
        return jnp.clip(j - j0, 0, nt - 1)

    in_specs = [pl.BlockSpec((tm, d), lambda i, j: (i, 0)),
                pl.BlockSpec((1, d), lambda i, j: (0, 0)),
                pl.BlockSpec((gr, 1, d), lambda i, j: (i, 0, 0)),
                pl.BlockSpec((gr, 1, d), lambda i, j: (i, 0, 0)),
                pl.BlockSpec((d, tn), lambda i, j: (0, j)),
                pl.BlockSpec((1, tn), lambda i, j: (0, j))]
    args = [x, g, sc, sh, w, gain]
    n_gate = 0
    if small is not None:
        ws, bs, n_gate = small
        in_specs += [pl.BlockSpec(ws.shape, lambda i, j: (0, 0)),
                     pl.BlockSpec(bs.shape, lambda i, j: (0, 0))]
        args += [ws, bs]

    out_specs, out_shapes, stack_outs = [], [], []
    j0 = 0
    for kind, nt in segs:
        width = nt * tn
        all_spec = pl.BlockSpec((tm, tn), functools.partial(
            lambda i, j, j0, nt: (i, clipj(j, j0, nt)), j0=j0, nt=nt))
        if kind in ("hn_split", "split"):
            stack_outs.append(len(out_specs))
            out_specs.append(pl.BlockSpec((None, tm, tn), functools.partial(
                lambda i, j, j0, nt: (layer, jnp.minimum(i, npt - 1),
                                      jnp.where(i < npt, clipj(j, j0, nt), nt - 1)), j0=j0, nt=nt)))
            out_shapes.append(jax.ShapeDtypeStruct((depth, n_prompt, width), F32))
            stack_outs.append(len(out_specs))
            out_specs.append(pl.BlockSpec((None, tm, tn), functools.partial(
                lambda i, j, j0, nt: (layer, jnp.maximum(i - npt, 0),
                                      jnp.where(i >= npt, clipj(j, j0, nt), 0)), j0=j0, nt=nt)))
            out_shapes.append(jax.ShapeDtypeStruct((depth, nst * tm, width), F32))
        out_specs.append(all_spec)
        out_shapes.append(jax.ShapeDtypeStruct((n, width), BF16))
        j0 += nt
    assert j0 == nt_total
    if small is not None:
        for _ in range(2):
            out_specs.append(pl.BlockSpec((tm, LANES), lambda i, j: (i, 0)))
            out_shapes.append(jax.ShapeDtypeStruct((n, LANES), F32))
    aliases = {}
    if stacks is not None:
        assert len(stacks) == len(stack_outs)
        for arr, o in zip(stacks, stack_outs):
            aliases[len(args)] = o
            in_specs.append(pl.BlockSpec(memory_space=pl.ANY))
            args.append(arr)

    return pl.pallas_call(
        functools.partial(_proj_kernel, segs=tuple(segs), grp=grp, dh=dh, npt=npt, with_small=n_gate,
                          n_alias=len(aliases)),
        grid=(n // tm, nt_total),
        in_specs=in_specs,
        out_specs=out_specs,
        out_shape=out_shapes,
        scratch_shapes=[pltpu.VMEM((tm, d), BF16)],
        input_output_aliases=aliases,
        compiler_params=_params("arbitrary", "arbitrary"),
        name=name,
    )(*args)


def _scan_kernel(ga_ref, gb_ref, init_ref, lcs_ref, bcum_ref, b_ref, carry):
    c = pl.program_id(1)

    @pl.when(c == 0)
    def _():
        carry[...] = init_ref[...]

    a = ga_ref[...]
    ln = a.shape[0]
    r = lax.broadcasted_iota(jnp.int32, (ln, ln), 0)
    cc = lax.broadcasted_iota(jnp.int32, (ln, ln), 1)
    tri = jnp.where(r >= cc, 1.0, 0.0).astype(BF16)
    hi, mid, lo = _split3(a)
    cs = _dot(tri, hi) + _dot(tri, mid) + _dot(tri, lo)
    lcs_ref[...] = cs + carry[...]
    bcum_ref[...] = cs
    b_ref[...] = gb_ref[...] - cs
    carry[...] = carry[...] + cs[ln - 1:ln, :]


def _scan_call(ga, gb, init, chunk, name):
    b, t, _ = ga.shape
    spec = pl.BlockSpec((None, chunk, LANES), lambda bi, c: (bi, c, 0))
    shp = jax.ShapeDtypeStruct((b, t, LANES), F32)
    return pl.pallas_call(
        _scan_kernel,
        grid=(b, t // chunk),
        in_specs=[spec, spec, pl.BlockSpec((None, 1, LANES), lambda bi, c: (bi, 0, 0))],
        out_specs=[spec, spec, spec],
        out_shape=[shp, shp, shp],
        scratch_shapes=[pltpu.VMEM((1, LANES), F32)],
        compiler_params=_params("arbitrary", "arbitrary"),
        name=name,
    )(ga, gb, init)


def _bias_lanes(x_row, own):
    x = jnp.broadcast_to(x_row, (LANES, x_row.shape[1])).T
    hi, mid, lo = _split3(x)
    lane = lax.broadcasted_iota(jnp.int32, x.shape, 1)
    base = 0 if own else 3
    const = jnp.where((lane >= 3 - base) & (lane < 6 - base), -1.0 if own else 1.0, 0.0).astype(BF16)
    return jnp.where(lane == base, hi, jnp.where(lane == base + 1, mid, jnp.where(lane == base + 2, lo, const)))


def _fox_kernel(q_ref, k_ref, v_ref, lrow_ref, o_ref, vt_scr, ka_scr, qa_scr, m_scr, l_scr, acc_scr,
                sa_scr, ca_scr, sb_scr, cb_scr, *, tb, nh, dh):
    qi = pl.program_id(2)
    nblk = k_ref.shape[0] // tb

    @pl.when(qi == 0)
    def _():
        def prep(c, carry):
            off = pl.multiple_of(c * tb, tb)
            for hh in range(nh):
                v = v_ref[pl.ds(off, tb), hh * dh:(hh + 1) * dh]
                vt_scr[hh, c] = v.astype(F32).T.astype(BF16)
                ka_scr[hh, c, :, :dh] = k_ref[pl.ds(off, tb), hh * dh:(hh + 1) * dh]
                ka_scr[hh, c, :, dh:] = _bias_lanes(lrow_ref[hh, c], False)
            return carry

        lax.fori_loop(0, nblk, prep, 0)

    for hh in range(nh):
        m_scr[hh] = jnp.full((1, tb), NEG_INF, F32)
        l_scr[hh] = jnp.zeros((1, tb), F32)
        acc_scr[hh] = jnp.zeros((dh, tb), F32)
        qa_scr[hh, :, :dh] = q_ref[:, hh * dh:(hh + 1) * dh]
        qa_scr[hh, :, dh:] = _bias_lanes(lrow_ref[hh, qi], True)

    def scores(ki, s_buf, c_buf, masked):
        for hh in range(nh):
            st = _dot_nt(ka_scr[hh, ki], qa_scr[hh])
            if masked:
                r = lax.broadcasted_iota(jnp.int32, (tb, tb), 0)
                c = lax.broadcasted_iota(jnp.int32, (tb, tb), 1)
                st = jnp.where(r <= c, st, NEG_INF)
            s_buf[hh] = st
            c_buf[hh] = jnp.max(st, axis=0, keepdims=True)

    def accumulate(ki, s_buf, c_buf):
        for hh in range(nh):
            m_prev = m_scr[hh]
            m_new = jnp.maximum(m_prev, c_buf[hh])
            p = jnp.exp2(s_buf[hh] - m_new)
            alpha = jnp.exp2(m_prev - m_new)
            l_scr[hh] = alpha * l_scr[hh] + jnp.sum(p, axis=0, keepdims=True)
            acc_scr[hh] = alpha * acc_scr[hh] + _dot(vt_scr[hh, ki], p.astype(BF16))
            m_scr[hh] = m_new

    scores(qi, sa_scr, ca_scr, True)

    def pair(tstep, carry):
        k1 = qi - 1 - 2 * tstep
        scores(k1, sb_scr, cb_scr, False)
        accumulate(k1 + 1, sa_scr, ca_scr)
        scores(k1 - 1, sa_scr, ca_scr, False)
        accumulate(k1, sb_scr, cb_scr)
        return carry

    lax.fori_loop(0, qi // 2, pair, 0)

    @pl.when(qi % 2 == 1)
    def _():
        scores(0, sb_scr, cb_scr, False)
        accumulate(1, sa_scr, ca_scr)
        accumulate(0, sb_scr, cb_scr)

    @pl.when(qi % 2 == 0)
    def _():
        accumulate(0, sa_scr, ca_scr)

    for hh in range(nh):
        o = acc_scr[hh] * (1.0 / l_scr[hh])
        o_ref[:, hh * dh:(hh + 1) * dh] = o.T.astype(o_ref.dtype)


def _fox_prompt_call(q, k, v, lrows, *, b, t, fh, dh, tb, nh):
    n = q.shape[0]
    nq = t // tb
    hw = nh * dh
    return pl.pallas_call(
        functools.partial(_fox_kernel, tb=tb, nh=nh, dh=dh),
        grid=(b, fh // nh, nq),
        in_specs=[pl.BlockSpec((tb, hw), lambda bi, h, qi: (bi * nq + qi, h)),
                  pl.BlockSpec((t, hw), lambda bi, h, qi: (bi, h)),
                  pl.BlockSpec((t, hw), lambda bi, h, qi: (bi, h)),
                  pl.BlockSpec((nh, nq, 1, tb), lambda bi, h, qi: (bi * (fh // nh) + h, 0, 0, 0))],
        out_specs=pl.BlockSpec((tb, hw), lambda bi, h, qi: (bi * nq + qi, h)),
        out_shape=jax.ShapeDtypeStruct((n, fh * dh), BF16),
        scratch_shapes=[pltpu.VMEM((nh, nq, dh, tb), BF16), pltpu.VMEM((nh, nq, tb, dh + LANES), BF16),
                        pltpu.VMEM((nh, tb, dh + LANES), BF16),
                        pltpu.VMEM((nh, 1, tb), F32), pltpu.VMEM((nh, 1, tb), F32),
                        pltpu.VMEM((nh, dh, tb), F32),
                        pltpu.VMEM((nh, tb, tb), F32), pltpu.VMEM((nh, 1, tb), F32),
                        pltpu.VMEM((nh, tb, tb), F32), pltpu.VMEM((nh, 1, tb), F32)],
        compiler_params=_params("arbitrary", "arbitrary", "arbitrary"),
        name="fox_prompt",
    )(q, k, v, lrows)


def _fox_sample_kernel(q_ref, kp_ref, vp_ref, kn_ref, vn_ref, lcs_ref, lkp_ref, lkn_ref, y_any, o_ref, *, fh, dh):
    del y_any
    ts = q_ref.shape[0]
    lcs = lcs_ref[...]
    r = lax.broadcasted_iota(jnp.int32, (ts, ts), 0)
    c = lax.broadcasted_iota(jnp.int32, (ts, ts), 1)
    for h in range(fh):
        cols = slice(h * dh, (h + 1) * dh)
        q = q_ref[:, cols]
        lq = _lane_select(lcs, h)[:, :1] * LOG2E
        sp = _dot_nt(q, kp_ref[:, h, :].astype(BF16)) + lq - lkp_ref[h:h + 1, :]
        sn = _dot_nt(q, kn_ref[:, cols]) + lq - lkn_ref[h:h + 1, :]
        sn = jnp.where(r >= c, sn, NEG_INF)
        m = jnp.maximum(jnp.max(sp, axis=1, keepdims=True), jnp.max(sn, axis=1, keepdims=True))
        pp = jnp.exp2(sp - m)
        pn = jnp.exp2(sn - m)
        den = jnp.sum(pp, axis=1, keepdims=True) + jnp.sum(pn, axis=1, keepdims=True)
        acc = _dot(pp.astype(BF16), vp_ref[:, h, :].astype(BF16)) + _dot(pn.astype(BF16), vn_ref[:, cols])
        o_ref[:, cols] = (acc / den).astype(o_ref.dtype)


def _fox_sample_call(y, q, kb, vb, cache_k, cache_v, lcs_s, lkp_rows, lkn_rows, *, layer, n_prompt, bs, ts, fh, dh):
    p = cache_k.shape[2]
    r0 = n_prompt // ts
    fw = fh * dh
    return pl.pallas_call(
        functools.partial(_fox_sample_kernel, fh=fh, dh=dh),
        grid=(bs,),
        in_specs=[pl.BlockSpec((ts, fw), lambda bi: (r0 + bi, 0)),
                  pl.BlockSpec((None, None, p, fh, dh), lambda bi: (layer, bi, 0, 0, 0)),
                  pl.BlockSpec((None, None, p, fh, dh), lambda bi: (layer, bi, 0, 0, 0)),
                  pl.BlockSpec((ts, fw), lambda bi: (r0 + bi, 0)),
                  pl.BlockSpec((ts, fw), lambda bi: (r0 + bi, 0)),
                  pl.BlockSpec((ts, LANES), lambda bi: (bi, 0)),
                  pl.BlockSpec((None, fh, p), lambda bi: (bi, 0, 0)),
                  pl.BlockSpec((None, fh, ts), lambda bi: (bi, 0, 0)),
                  pl.BlockSpec(memory_space=pl.ANY)],
        out_specs=pl.BlockSpec((ts, fw), lambda bi: (r0 + bi, 0)),
        out_shape=jax.ShapeDtypeStruct(y.shape, y.dtype),
        input_output_aliases={8: 0},
        compiler_params=_params("arbitrary"),
        name="fox_sample",
    )(q, cache_k, cache_v, kb, vb, lcs_s, lkp_rows, lkn_rows, y)


def _mlstm_kernel(*refs, hoff, aliased):
    if aliased:
        refs = refs[:11] + refs[12:]
    (q_ref, k_ref, v_ref, o_ref, bcum_ref, bnat_ref, brow_ref, g_ref, c0_ref, n0_ref, m0_ref,
     y_ref, c1_ref, n1_ref, m1_ref, c_scr, n_scr, m_scr) = refs
    h = pl.program_id(1)
    c = pl.program_id(2)
    nc = pl.num_programs(2)

    @pl.when(c == 0)
    def _():
        c_scr[...] = c0_ref[...]
        n_scr[...] = n0_ref[...]
        m_scr[...] = m0_ref[...]

    q = q_ref[...]
    k = k_ref[...]
    v = v_ref[...]
    ln = q.shape[0]
    bcum = _lane_select(bcum_ref[...], hoff + h)[:, :1]
    bcol = _lane_select(bnat_ref[...], hoff + h)[:, :1]
    brow = brow_ref[...]
    m0 = m_scr[:, :1]
    r = lax.broadcasted_iota(jnp.int32, (ln, ln), 0)
    cc = lax.broadcasted_iota(jnp.int32, (ln, ln), 1)
    bm = jnp.where(r >= cc, brow, NEG_INF)
    g = jnp.maximum(m0, jnp.max(bm, axis=1, keepdims=True))
    dmat = jnp.exp(bm - g)
    s = _dot_nt(q, k) * dmat
    w_inter = jnp.exp(m0 - g)
    cmat = c_scr[...]
    num = _dot(s.astype(BF16), v) + _dot(q, cmat.astype(BF16)) * w_inter
    qn = jnp.sum(q.astype(F32) * n_scr[...], axis=1, keepdims=True)
    den = jnp.sum(s, axis=1, keepdims=True) + w_inter * qn
    den = jnp.maximum(jnp.abs(den), jnp.exp(-(bcum + g)))
    hval = num * (1.0 / den)
    ms = jnp.mean(hval * hval, axis=1, keepdims=True)
    y = hval * lax.rsqrt(ms + EPS) * g_ref[...] * o_ref[...].astype(F32)
    y_ref[...] = y.astype(y_ref.dtype)

    g_last = g[ln - 1:ln, :]
    w_key = jnp.exp(bcol - g_last)
    f_tot = jnp.exp(m0 - g_last)
    kw = k.astype(F32) * w_key
    c_new = f_tot * cmat + _dot(kw.T.astype(BF16), v)
    n_new = f_tot * n_scr[...] + jnp.sum(kw, axis=0, keepdims=True)
    m_new = jnp.broadcast_to(bcum[ln - 1:ln, :] + g_last, m_scr.shape)
    c_scr[...] = c_new
    n_scr[...] = n_new
    m_scr[...] = m_new

    @pl.when(c == nc - 1)
    def _():
        c1_ref[...] = c_new
        n1_ref[...] = n_new
        m1_ref[...] = m_new


def _mlstm_call(mq, mk, mv, mo, bcum, bnat, brows, gain, c0, n0, m0, *, row0, nb, nc, chunk, mh, dqk, dv,
                hoff, y_prev, name):
    n = mq.shape[0]
    rb0 = row0 // chunk

    def rows(bi, h, c):
        return rb0 + bi * nc + c

    in_specs = [pl.BlockSpec((chunk, dqk), lambda bi, h, c: (rows(bi, h, c), h)),
                pl.BlockSpec((chunk, dqk), lambda bi, h, c: (rows(bi, h, c), h)),
                pl.BlockSpec((chunk, dv), lambda bi, h, c: (rows(bi, h, c), h)),
                pl.BlockSpec((chunk, dv), lambda bi, h, c: (rows(bi, h, c), h)),
                pl.BlockSpec((chunk, LANES), lambda bi, h, c: (bi * nc + c, 0)),
                pl.BlockSpec((chunk, LANES), lambda bi, h, c: (bi * nc + c, 0)),
                pl.BlockSpec((None, None, 1, chunk), lambda bi, h, c: (bi * mh + h, c, 0, 0)),
                pl.BlockSpec((None, 1, dv), lambda bi, h, c: (h, 0, 0)),
                pl.BlockSpec((None, None, dqk, dv), lambda bi, h, c: (bi, h, 0, 0)),
                pl.BlockSpec((None, None, 1, dqk), lambda bi, h, c: (bi, h, 0, 0)),
                pl.BlockSpec((None, None, 1, LANES), lambda bi, h, c: (bi, h, 0, 0))]
    args = [mq, mk, mv, mo, bcum, bnat, brows, gain, c0, n0, m0]
    aliases = {}
    if y_prev is not None:
        in_specs.append(pl.BlockSpec(memory_space=pl.ANY))
        args.append(y_prev)
        aliases = {len(args) - 1: 0}
    return pl.pallas_call(
        functools.partial(_mlstm_kernel, hoff=hoff, aliased=y_prev is not None),
        grid=(nb, mh, nc),
        in_specs=in_specs,
        out_specs=[pl.BlockSpec((chunk, dv), lambda bi, h, c: (rows(bi, h, c), h)),
                   pl.BlockSpec((None, None, dqk, dv), lambda bi, h, c: (bi, h, 0, 0)),
                   pl.BlockSpec((None, None, 1, dqk), lambda bi, h, c: (bi, h, 0, 0)),
                   pl.BlockSpec((None, None, 1, LANES), lambda bi, h, c: (bi, h, 0, 0))],
        out_shape=[jax.ShapeDtypeStruct((n, mh * dv), BF16),
                   jax.ShapeDtypeStruct((nb, mh, dqk, dv), F32),
                   jax.ShapeDtypeStruct((nb, mh, 1, dqk), F32),
                   jax.ShapeDtypeStruct((nb, mh, 1, LANES), F32)],
        scratch_shapes=[pltpu.VMEM((dqk, dv), F32), pltpu.VMEM((1, dqk), F32), pltpu.VMEM((1, LANES), F32)],
        input_output_aliases=aliases,
        compiler_params=_params("arbitrary", "arbitrary", "arbitrary"),
        name=name,
    )(*args)


def _route(logits, b_router, n_exp, epg):
    n_grp = n_exp // epg
    lane = lax.broadcasted_iota(jnp.int32, logits.shape, 1)
    valid = lane < n_exp
    pos = lane % epg
    grp = lane // epg
    s = _sigmoid(logits)
    sb = jnp.where(valid, s + b_router, NEG_INF)

    def nbr(x, d):
        return pltpu.roll(x, (-d) % LANES, axis=1)

    def count_beaten(x, idx, span, step):
        cnt = jnp.zeros(x.shape, F32)
        for d in range(-(span - 1), span):
            if d == 0:
                continue
            other = nbr(x, d * step)
            inside = (idx + d >= 0) & (idx + d < span)
            ahead = (other > x) | ((other == x) & (d < 0))
            cnt = cnt + jnp.where(inside & ahead, 1.0, 0.0)
        return cnt

    top2 = count_beaten(sb, pos, epg, 1) < 2.0
    t2v = jnp.where(top2, sb, 0.0)
    t2v = jnp.where(valid, t2v, NEG_INF)
    gscore = t2v
    for d in range(-(epg - 1), epg):
        if d == 0:
            continue
        inside = (pos + d >= 0) & (pos + d < epg)
        gscore = gscore + jnp.where(inside, nbr(t2v, d), 0.0)
    best_grp = count_beaten(gscore, grp, n_grp, epg) < 1.0
    sel = valid & best_grp & top2
    w = jnp.where(sel, s, 0.0)
    return w / jnp.sum(w, axis=1, keepdims=True), jnp.where(sel, 1.0, 0.0)


def _merge_kernel(x_ref, g1n_ref, sc1_ref, sh1_ref, gt1_ref, yf_ref, ym_ref, wga_ref, wgb_ref, bga_ref, bgb_ref,
                  wo_ref, g2n_ref, sc2_ref, sh2_ref, wrh_ref, wrl_ref, br_ref,
                  x1_ref, h2_ref, gates_ref, sel_ref, h_scr, acc_scr, *, grp, n_exp, epg):
    j = pl.program_id(1)
    nj = pl.num_programs(1)

    @pl.when(j == 0)
    def _():
        h_scr[...] = _norm_mod(x_ref[...], g1n_ref[...], sc1_ref[...], sh1_ref[...], grp).astype(BF16)
        acc_scr[...] = jnp.zeros(acc_scr.shape, F32)

    mb = min(h_scr.shape[0], PROJ_ROW_BATCH)
    for rb in range(h_scr.shape[0] // mb):
        rows = slice(rb * mb, (rb + 1) * mb)
        h = h_scr[rows, :]
        ga = _dot(h, wga_ref[...]) + bga_ref[...]
        gb = _dot(h, wgb_ref[...]) + bgb_ref[...]
        mix = _sigmoid(ga) * yf_ref[rows, :].astype(F32) + _sigmoid(gb) * ym_ref[rows, :].astype(F32)
        acc_scr[rows, :] += _dot(mix.astype(BF16), wo_ref[...])

    @pl.when(j == nj - 1)
    def _():
        tm, d = x_ref.shape
        acc3 = acc_scr[...].reshape(tm // grp, grp, d) * gt1_ref[...]
        x1 = x_ref[...] + acc3.reshape(tm, d)
        x1_ref[...] = x1
        h2 = _norm_mod(x1, g2n_ref[...], sc2_ref[...], sh2_ref[...], grp)
        hi = h2.astype(BF16)
        lo = (h2 - hi.astype(F32)).astype(BF16)
        h2_ref[...] = h2
        logits = _dot(hi, wrh_ref[...]) + _dot(lo, wrh_ref[...]) + _dot(hi, wrl_ref[...])
        gates_ref[...], sel_ref[...] = _route(logits, br_ref[...], n_exp, epg)


def _merge_call(x, g1n, sc1, sh1, gt1, yf, ym, wgate, bgate, wout, g2n, sc2, sh2, wr_hi, wr_lo, br,
                *, tm, tc, grp, n_exp, epg):
    n, d = x.shape
    nj = d // tc
    gr = tm // grp
    row = pl.BlockSpec((tm, d), lambda i, j: (i, 0))
    vec = pl.BlockSpec((1, d), lambda i, j: (0, 0))
    mod = pl.BlockSpec((gr, 1, d), lambda i, j: (i, 0, 0))
    return pl.pallas_call(
        functools.partial(_merge_kernel, grp=grp, n_exp=n_exp, epg=epg),
        grid=(n // tm, nj),
        in_specs=[row, vec, mod, mod, mod,
                  pl.BlockSpec((tm, tc), lambda i, j: (i, j)),
                  pl.BlockSpec((tm, tc), lambda i, j: (i, j)),
                  pl.BlockSpec((d, tc), lambda i, j: (0, j)),
                  pl.BlockSpec((d, tc), lambda i, j: (0, nj + j)),
                  pl.BlockSpec((1, tc), lambda i, j: (0, j)),
                  pl.BlockSpec((1, tc), lambda i, j: (0, nj + j)),
                  pl.BlockSpec((tc, d), lambda i, j: (j, 0)),
                  vec, mod, mod,
                  pl.BlockSpec((d, LANES), lambda i, j: (0, 0)),
                  pl.BlockSpec((d, LANES), lambda i, j: (0, 0)),
                  pl.BlockSpec((1, LANES), lambda i, j: (0, 0))],
        out_specs=[row, row, pl.BlockSpec((tm, LANES), lambda i, j: (i, 0)),
                   pl.BlockSpec((tm, LANES), lambda i, j: (i, 0))],
        out_shape=[jax.ShapeDtypeStruct((n, d), F32), jax.ShapeDtypeStruct((n, d), F32),
                   jax.ShapeDtypeStruct((n, LANES), F32), jax.ShapeDtypeStruct((n, LANES), F32)],
        scratch_shapes=[pltpu.VMEM((tm, d), BF16), pltpu.VMEM((tm, d), F32)],
        compiler_params=_params("arbitrary", "arbitrary"),
        name="merge_out",
    )(x, g1n, sc1, sh1, gt1, yf, ym, wgate, wgate, bgate, bgate, wout, g2n, sc2, sh2, wr_hi, wr_lo, br)


def _slot_table_kernel(p0_ref, p1_ref, lo_ref, hi_ref, idx_ref, *, n_tok):
    for k in range(lo_ref.shape[0]):
        def clear(r, c):
            idx_ref[r] = 0
            return c

        lax.fori_loop(lo_ref[k], hi_ref[k], clear, 0)

    def fill(r, c):
        idx_ref[p0_ref[r]] = r
        idx_ref[p1_ref[r]] = r
        return c

    lax.fori_loop(0, n_tok, fill, 0, unroll=8)


def _slot_table_call(pos0, pos1, pad_lo, pad_hi, n_slot):
    n_tok = pos0.shape[0]
    smem = pl.BlockSpec(memory_space=pltpu.SMEM)
    return pl.pallas_call(
        functools.partial(_slot_table_kernel, n_tok=n_tok),
        in_specs=[smem, smem, smem, smem],
        out_specs=smem,
        out_shape=jax.ShapeDtypeStruct((n_slot,), jnp.int32),
        name="moe_slots",
    )(pos0, pos1, pad_lo, pad_hi)


def _gather_rows(src_any, idx_ref, base, dst, sem, n_rows, inline=False):
    def issue(r, c):
        pltpu.make_async_copy(src_any.at[pl.ds(idx_ref[base + r], 1)], dst.at[pl.ds(r, 1)], sem).start()
        return c

    if inline:
        for r in range(n_rows):
            issue(r, 0)
    else:
        lax.fori_loop(0, n_rows, issue, 0, unroll=8)


def _wait_rows(src_any, dst, sem):
    pltpu.make_async_copy(src_any.at[pl.ds(0, dst.shape[0])], dst, sem).wait()


def _expert_kernel(idx_ref, te_ref, nu_ref, h_any, wg_ref, wu_ref, wd_ref, ys_ref,
                   xbuf, wgb, wub, wdb, sem, *, tmx):
    t = pl.program_id(0)
    nt = pl.num_programs(0)
    ahead = GATHER_SLOTS - 1
    slot = t % GATHER_SLOTS

    @pl.when(t == 0)
    def _():
        for k in range(ahead):
            _gather_rows(h_any, idx_ref, jnp.minimum(k, nt - 1) * tmx, xbuf.at[k], sem.at[k], tmx)

    @pl.when((t == 0) | (te_ref[t] != te_ref[jnp.maximum(t - 1, 0)]))
    def _():
        wgb[...] = wg_ref[...].astype(BF16)
        wub[...] = wu_ref[...].astype(BF16)
        wdb[...] = wd_ref[...].astype(BF16)

    _wait_rows(h_any, xbuf.at[slot], sem.at[slot])
    nxt = jnp.minimum(t + ahead, nt - 1) * tmx
    nslot = (t + ahead) % GATHER_SLOTS

    @pl.when(t < nu_ref[0])
    def _():
        x = xbuf[slot].astype(BF16)
        a = _dot(x, wgb[...])
        u = _dot(x, wub[...])
        y = (a * _sigmoid(a)) * u
        ys_ref[...] = _dot(y.astype(BF16), wdb[...])
        _gather_rows(h_any, idx_ref, nxt, xbuf.at[nslot], sem.at[nslot], tmx, inline=True)

    @pl.when(t >= nu_ref[0])
    def _():
        ys_ref[...] = jnp.zeros(ys_ref.shape, F32)
        _gather_rows(h_any, idx_ref, nxt, xbuf.at[nslot], sem.at[nslot], tmx)

    @pl.when(t == nt - 1)
    def _():
        for k in range(1, GATHER_SLOTS):
            s = (t + k) % GATHER_SLOTS
            _wait_rows(h_any, xbuf.at[s], sem.at[s])


def _expert_call(idx, tile_expert, n_used, h2, wg, wu, wd, *, tmx, layer):
    n, d = h2.shape
    de = wg.shape[-1]
    nt = tile_expert.shape[0]
    grid_spec = pltpu.PrefetchScalarGridSpec(
        num_scalar_prefetch=3,
        grid=(nt,),
        in_specs=[pl.BlockSpec(memory_space=pl.ANY),
                  pl.BlockSpec((None, None, d, de), lambda t, idx, te, nu: (layer, te[t], 0, 0)),
                  pl.BlockSpec((None, None, d, de), lambda t, idx, te, nu: (layer, te[t], 0, 0)),
                  pl.BlockSpec((None, None, de, d), lambda t, idx, te, nu: (layer, te[t], 0, 0))],
        out_specs=pl.BlockSpec((tmx, d), lambda t, idx, te, nu: (t, 0)),
        scratch_shapes=[pltpu.VMEM((GATHER_SLOTS, tmx, d), F32), pltpu.VMEM((d, de), BF16),
                        pltpu.VMEM((d, de), BF16), pltpu.VMEM((de, d), BF16),
                        pltpu.SemaphoreType.DMA((GATHER_SLOTS,))])
    return pl.pallas_call(
        functools.partial(_expert_kernel, tmx=tmx),
        grid_spec=grid_spec,
        out_shape=jax.ShapeDtypeStruct((nt * tmx, d), F32),
        compiler_params=_params("arbitrary"),
        name="moe_experts",
    )(idx, tile_expert, n_used, h2, wg, wu, wd)


def _combine_kernel(p0_ref, p1_ref, ys_any, x1_ref, gt2_ref, wp_ref, *refs, tmc, grp, npt):
    out_refs, (buf, sem) = refs[:-2], refs[-2:]
    i = pl.program_id(0)
    nt = pl.num_programs(0)
    slot = i % 2

    def gather(base, s, inline):
        _gather_rows(ys_any, p0_ref, base, buf.at[s, 0], sem.at[s], tmc, inline)
        _gather_rows(ys_any, p1_ref, base, buf.at[s, 1], sem.at[s], tmc, inline)

    def wait(s):
        _wait_rows(ys_any, buf.at[s, 0], sem.at[s])
        _wait_rows(ys_any, buf.at[s, 1], sem.at[s])

    @pl.when(i == 0)
    def _():
        gather(0, 0, False)

    wait(slot)
    gather(jnp.minimum(i + 1, nt - 1) * tmc, 1 - slot, True)
    w = wp_ref[...]
    moe = w[:, 0:1] * buf[slot, 0] + w[:, 1:2] * buf[slot, 1]
    tm, d = moe.shape
    out = x1_ref[...] + (moe.reshape(tm // grp, grp, d) * gt2_ref[...]).reshape(tm, d)
    if len(out_refs) == 1:
        out_refs[0][...] = out
    else:
        @pl.when(i < npt)
        def _():
            out_refs[0][...] = out

        @pl.when(i >= npt)
        def _():
            out_refs[1][...] = out

    @pl.when(i == nt - 1)
    def _():
        wait(1 - slot)


def _combine_call(pos0, pos1, ys, x1, gt2, wpair, *, tmc, grp, n_prompt, split):
    n, d = x1.shape
    gr = tmc // grp
    npt = n_prompt // tmc
    row = lambda i, p0, p1: (i, 0)
    if split:
        out_specs = [pl.BlockSpec((tmc, d), lambda i, p0, p1: (jnp.minimum(i, npt - 1), 0)),
                     pl.BlockSpec((tmc, d), lambda i, p0, p1: (jnp.maximum(i - npt, 0), 0))]
        out_shape = [jax.ShapeDtypeStruct((n_prompt, d), F32), jax.ShapeDtypeStruct((n - n_prompt, d), F32)]
    else:
        out_specs = [pl.BlockSpec((tmc, d), row)]
        out_shape = [jax.ShapeDtypeStruct((n, d), F32)]
    grid_spec = pltpu.PrefetchScalarGridSpec(
        num_scalar_prefetch=2,
        grid=(n // tmc,),
        in_specs=[pl.BlockSpec(memory_space=pl.ANY),
                  pl.BlockSpec((tmc, d), row),
                  pl.BlockSpec((gr, 1, d), lambda i, p0, p1: (i, 0, 0)),
                  pl.BlockSpec((tmc, LANES), row)],
        out_specs=out_specs,
        scratch_shapes=[pltpu.VMEM((2, 2, tmc, d), F32), pltpu.SemaphoreType.DMA((2,))])
    return pl.pallas_call(
        functools.partial(_combine_kernel, tmc=tmc, grp=grp, npt=npt),
        grid_spec=grid_spec,
        out_shape=out_shape,
        compiler_params=_params("arbitrary"),
        name="moe_combine",
    )(pos0, pos1, ys, x1, gt2, wpair)


def _moe_sparse(h2, gates, sel, x1, gt2, wg, wu, wd, *, layer, tmx, tmc, grp, n_prompt, split):
    n, d = x1.shape
    n_exp = wg.shape[1]
    nt = -(-(2 * n + n_exp * (tmx - 1)) // tmx)
    n_slot = nt * tmx
    cs = _scan_call(sel.reshape(1, n, LANES), sel.reshape(1, n, LANES), jnp.zeros((1, 1, LANES), F32),
                    _pick([n], (256, 128, 64)), "scan_route")[0].reshape(n, LANES)
    counts = cs[n - 1, :n_exp].astype(jnp.int32)
    padded = (counts + tmx - 1) // tmx * tmx
    ends = jnp.cumsum(padded)
    off = jnp.zeros((LANES,), jnp.int32).at[:n_exp].set(ends - padded)
    dest = off[None, :] + cs.astype(jnp.int32) - 1
    chosen = sel > 0.0
    pos0 = jnp.min(jnp.where(chosen, dest, n_slot), axis=1)
    pos1 = jnp.max(jnp.where(chosen, dest, -1), axis=1)
    w0 = jnp.sum(jnp.where(chosen & (dest == pos0[:, None]), gates, 0.0), axis=1)
    w1 = jnp.sum(jnp.where(chosen & (dest == pos1[:, None]), gates, 0.0), axis=1)
    wpair = jnp.zeros((n, LANES), F32).at[:, 0].set(w0).at[:, 1].set(w1)
    tile_expert = jnp.minimum(jnp.sum(jnp.arange(nt)[:, None] * tmx >= ends[None, :], axis=1), n_exp - 1)
    n_used = (ends[n_exp - 1] // tmx).reshape(1).astype(jnp.int32)
    pad_lo = jnp.concatenate([ends - padded + counts, ends[n_exp - 1:]]).astype(jnp.int32)
    pad_hi = jnp.concatenate([ends, jnp.full((1,), n_slot, jnp.int32)]).astype(jnp.int32)
    idx = _slot_table_call(pos0, pos1, pad_lo, pad_hi, n_slot)
    ys = _expert_call(idx, tile_expert.astype(jnp.int32), n_used, h2, wg, wu, wd, tmx=tmx, layer=layer)
    return _combine_call(pos0, pos1, ys, x1, gt2, wpair, tmc=tmc, grp=grp, n_prompt=n_prompt, split=split)


def kernel(x_prompt, x_sample, cache_fox_k, cache_fox_v, cache_fox_logf, state_mlstm_c, state_mlstm_n, state_mlstm_m, c_prompt, c_sample, norm1_g, norm2_g, w_ada, b_ada, w_in, fox_qn_g, fox_kn_g, fox_fb, ml_ib, ml_fb, ml_out_g, w_gate, b_gate, w_out, w_router, b_router, w_e_gate, w_e_up, w_e_down):
    b, t, d = x_prompt.shape
    bs, ts, _ = x_sample.shape
    depth = w_in.shape[0]
    past = cache_fox_k.shape[2]
    fh = fox_fb.shape[-1]
    dh = d // fh
    fw = fh * dh
    mh = ml_ib.shape[-1]
    dqk, dv = state_mlstm_c.shape[-2:]
    qw, vw = mh * dqk, mh * dv
    n_exp = w_router.shape[-1]
    epg = n_exp // 4
    n_p, n_s = b * t, bs * ts
    n = n_p + n_s
    grp = ts
    tm_a = _pick([n_p, n_s], (1024, 512, 256, 128))
    tm_b = _pick([n_p, n_s], (512, 256, 128))
    tn = _pick([fw, qw, vw], (512, 256, 128))
    tc = _pick([d], (512, 256, 128))
    tb = _pick([t], (512, 256, 128))
    nh = 2 if fh % 2 == 0 else 1
    tmx = 256
    tmc = _pick([n_p, n_s], (256, 128))
    lp = _pick([t], (256, 128, 64))
    pc = _pick([past], (256, 128, 64))
    assert fh + mh <= LANES and n_exp <= LANES and t % grp == 0 and tm_b % grp == 0

    x = jnp.concatenate([x_prompt.reshape(n_p, d), x_sample.reshape(n_s, d)], axis=0)
    n_seq = b + bs
    r_pad = -(-n_seq // 8) * 8
    c_all = jnp.concatenate([c_prompt, c_sample, jnp.zeros((r_pad - n_seq, d), F32)], axis=0)
    mod = _ada_call(c_all, w_ada, b_ada)
    seq_of_group = jnp.concatenate([jnp.repeat(jnp.arange(b), t // grp), b + jnp.arange(bs)])

    cuts = [0]
    for wdt in (fw, fw, fw, fh, qw, qw, vw, mh, mh, vw):
        cuts.append(cuts[-1] + wdt)
    (c_fq, c_fk, c_fv, c_ff, c_mq, c_mk, c_mv, c_mi, c_mf, c_mo, _) = cuts

    wr_hi = jnp.zeros((d, LANES), F32).at[:, :n_exp].set(w_router)
    wr_lo = (wr_hi - wr_hi.astype(BF16).astype(F32)).astype(BF16)
    wr_hi = wr_hi.astype(BF16)
    br = jnp.zeros((1, LANES), F32).at[0, :n_exp].set(b_router)
    zeros_c = jnp.zeros((b, mh, dqk, dv), F32)
    zeros_n = jnp.zeros((b, mh, 1, dqk), F32)
    zeros_m = jnp.zeros((b, mh, 1, LANES), F32)

    outs_p, outs_s = [], []
    kv_stacks = None
    for l in range(depth):
        modl = mod[l][seq_of_group]
        sh1, sc1, gt1, sh2, sc2, gt2 = [modl[:, k * d:(k + 1) * d].reshape(n // grp, 1, d) for k in range(6)]
        wl = w_in[l]
        w1 = wl[:, :c_ff].astype(BF16)
        w2 = jnp.concatenate([wl[:, c_mq:c_mk] * (dqk ** -0.5), wl[:, c_mk:c_mi], wl[:, c_mo:]], axis=1).astype(BF16)
        ws = jnp.zeros((d, 2 * LANES), F32)
        ws = ws.at[:, :fh].set(wl[:, c_ff:c_mq]).at[:, fh:fh + mh].set(wl[:, c_mf:c_mo])
        ws = ws.at[:, LANES + fh:LANES + fh + mh].set(wl[:, c_mi:c_mf]).astype(BF16)
        bsm = jnp.zeros((1, 2 * LANES), F32)
        bsm = bsm.at[0, :fh].set(fox_fb[l]).at[0, fh:fh + mh].set(ml_fb[l])
        bsm = bsm.at[0, LANES + fh:LANES + fh + mh].set(ml_ib[l])
        gain1 = jnp.concatenate([jnp.tile(fox_qn_g[l] * (LOG2E * dh ** -0.5), fh), jnp.tile(fox_kn_g[l], fh),
                                 jnp.ones((fw,), F32)]).reshape(1, 3 * fw)
        g1n = norm1_g[l].reshape(1, d)
        g2n = norm2_g[l].reshape(1, d)

        q_b, k_p, k_s, k_b, v_p, v_s, v_b, g_a, g_b = _proj_call(
            x, g1n, sc1, sh1, w1, gain1, [("hn", fw // tn), ("hn_split", fw // tn), ("split", fw // tn)],
            tm=tm_a, tn=tn, grp=grp, dh=dh, n_prompt=n_p, small=(ws, bsm, fh + mh), name="proj_fox",
            layer=l, depth=depth, stacks=kv_stacks)
        kv_stacks = [k_p, k_s, v_p, v_s]
        m_q, m_k, m_v, m_o = _proj_call(
            x, g1n, sc1, sh1, w2, jnp.ones((1, w2.shape[1]), F32),
            [("bf16", qw // tn), ("bf16", qw // tn), ("bf16", vw // tn), ("sig", vw // tn)],
            tm=tm_a, tn=tn, grp=grp, dh=dh, n_prompt=n_p, name="proj_mlstm")

        lcs_p, bcum_p, bnat_p = _scan_call(g_a[:n_p].reshape(b, t, LANES), g_b[:n_p].reshape(b, t, LANES),
                                           jnp.zeros((b, 1, LANES), F32), lp, "scan_prompt")
        past_lf = jnp.zeros((bs, past, LANES), F32).at[:, :, :fh].set(cache_fox_logf[l])
        lcs_c, _, _ = _scan_call(past_lf, past_lf, jnp.zeros((bs, 1, LANES), F32), pc, "scan_cache")
        lcs_s, bcum_s, bnat_s = _scan_call(g_a[n_p:].reshape(bs, ts, LANES), g_b[n_p:].reshape(bs, ts, LANES),
                                           lcs_c[:, past - 1:past, :], ts, "scan_sample")

        lk_p = (jnp.transpose(lcs_p[:, :, :fh], (0, 2, 1)) * LOG2E).reshape(b * fh, t // tb, 1, tb)
        y_fox = _fox_prompt_call(q_b, k_b, v_b, lk_p, b=b, t=t, fh=fh, dh=dh, tb=tb, nh=nh)
        lk_c = jnp.transpose(lcs_c[:, :, :fh], (0, 2, 1)) * LOG2E
        lk_n = jnp.transpose(lcs_s[:, :, :fh], (0, 2, 1)) * LOG2E
        y_fox = _fox_sample_call(y_fox, q_b, k_b, v_b, cache_fox_k, cache_fox_v, lcs_s.reshape(n_s, LANES),
                                 lk_c, lk_n, layer=l, n_prompt=n_p, bs=bs, ts=ts, fh=fh, dh=dh)

        gain_m = ml_out_g[l].reshape(mh, 1, dv)
        br_p = jnp.transpose(bnat_p[:, :, fh:fh + mh], (0, 2, 1)).reshape(b * mh, t // lp, 1, lp)
        y_ml, c1_p, n1_p, m1_p = _mlstm_call(
            m_q, m_k, m_v, m_o, bcum_p.reshape(n_p, LANES), bnat_p.reshape(n_p, LANES), br_p, gain_m,
            zeros_c, zeros_n, zeros_m, row0=0, nb=b, nc=t // lp, chunk=lp, mh=mh, dqk=dqk, dv=dv, hoff=fh,
            y_prev=None, name="mlstm_prompt")
        br_s = jnp.transpose(bnat_s[:, :, fh:fh + mh], (0, 2, 1)).reshape(bs * mh, 1, 1, ts)
        y_ml, c1_s, n1_s, m1_s = _mlstm_call(
            m_q, m_k, m_v, m_o, bcum_s.reshape(n_s, LANES), bnat_s.reshape(n_s, LANES), br_s, gain_m,
            state_mlstm_c[l], state_mlstm_n[l].reshape(bs, mh, 1, dqk),
            jnp.broadcast_to(state_mlstm_m[l][:, :, None, None], (bs, mh, 1, LANES)),
            row0=n_p, nb=bs, nc=1, chunk=ts, mh=mh, dqk=dqk, dv=dv, hoff=fh, y_prev=y_ml, name="mlstm_sample")

        x1, h2, gates, sel = _merge_call(
            x, g1n, sc1, sh1, gt1, y_fox, y_ml, w_gate[l].astype(BF16), b_gate[l].reshape(1, 2 * d),
            w_out[l].astype(BF16), g2n, sc2, sh2, wr_hi, wr_lo, br, tm=tm_b, tc=tc, grp=grp, n_exp=n_exp, epg=epg)
        moe_out = _moe_sparse(h2, gates, sel, x1, gt2, w_e_gate, w_e_up, w_e_down, layer=l, tmx=tmx, tmc=tmc,
                              grp=grp, n_prompt=n_p, split=l == depth - 1)
        x = moe_out[0]

        outs_p.append((g_a[:n_p, :fh].reshape(b, t, fh), c1_p, n1_p.reshape(b, mh, dqk), m1_p[:, :, 0, 0]))
        outs_s.append((g_a[n_p:, :fh].reshape(bs, ts, fh), c1_s, n1_s.reshape(bs, mh, dqk), m1_s[:, :, 0, 0]))

    def stack(outs, i):
        return jnp.stack([o[i] for o in outs], axis=0)

    k_p, k_s, v_p, v_s = kv_stacks
    return (moe_out[0].reshape(b, t, d), moe_out[1].reshape(bs, ts, d),
            k_p.reshape(depth, b, t, fh, dh), v_p.reshape(depth, b, t, fh, dh),
            stack(outs_p, 0), stack(outs_p, 1), stack(outs_p, 2), stack(outs_p, 3),
            k_s.reshape(depth, bs, ts, fh, dh), v_s.reshape(depth, bs, ts, fh, dh),
            stack(outs_s, 0), stack(outs_s, 1), stack(outs_s, 2), stack(outs_s, 3))
```

```python
import functools

import jax
import jax.numpy as jnp
from jax import lax
from jax.experimental import pallas as pl
from jax.experimental.pallas import tpu as pltpu

EPS = 1e-6
LANES = 128
VMEM_LIMIT = 56 * 1024 * 1024
BF16 = jnp.bfloat16
F32 = jnp.float32
NEG_INF = float("-inf")
LOG2E = 1.4426950408889634
GATHER_SLOTS = 3
PROJ_ROW_BATCH = 256


def _dot(a, b):
    return jnp.dot(a, b, preferred_element_type=F32)


def _dot_nt(a, b):
    return lax.dot_general(a, b, (((1,), (1,)), ((), ())), preferred_element_type=F32)


def _split3(x):
    hi = x.astype(BF16)
    r = x - hi.astype(F32)
    mid = r.astype(BF16)
    lo = (r - mid.astype(F32)).astype(BF16)
    return hi, mid, lo


def _lane_select(x, idx):
    rows = lax.broadcasted_iota(jnp.int32, (LANES, LANES), 0)
    onehot = jnp.where(rows == idx, 1.0, 0.0).astype(BF16)
    hi, mid, lo = _split3(x)
    return _dot(hi, onehot) + _dot(mid, onehot) + _dot(lo, onehot)


def _log_sigmoid(x):
    return jnp.minimum(x, 0.0) - jnp.log1p(jnp.exp(-jnp.abs(x)))


def _sigmoid(x):
    return 1.0 / (1.0 + jnp.exp(-x))


def _params(*sem, flags=None):
    return pltpu.CompilerParams(dimension_semantics=sem, vmem_limit_bytes=VMEM_LIMIT, flags=flags)


def _pick(n_list, cands):
    for c in cands:
        if all(n % c == 0 for n in n_list):
            return c
    raise ValueError(f"no tile in {cands} divides {n_list}")


def _ada_kernel(c_ref, w_ref, b_ref, o_ref):
    c = c_ref[...]
    sc = (c * _sigmoid(c)).astype(BF16)
    o_ref[...] = _dot(sc, w_ref[...].astype(BF16)) + b_ref[...]


def _ada_call(c_all, w_ada, b_ada):
    depth, d, n6 = w_ada.shape
    r = c_all.shape[0]
    tn = _pick([n6], (1024, 512, 256, 128))
    return pl.pallas_call(
        _ada_kernel,
        grid=(depth, n6 // tn),
        in_specs=[pl.BlockSpec((r, d), lambda l, j: (0, 0)),
                  pl.BlockSpec((None, d, tn), lambda l, j: (l, 0, j)),
                  pl.BlockSpec((None, 1, tn), lambda l, j: (l, 0, j))],
        out_specs=pl.BlockSpec((None, r, tn), lambda l, j: (l, 0, j)),
        out_shape=jax.ShapeDtypeStruct((depth, r, n6), F32),
        compiler_params=_params("arbitrary", "arbitrary"),
        name="ada_mod",
    )(c_all, w_ada, b_ada.reshape(depth, 1, n6))


def _norm_mod(x, g, sc, sh, grp):
    tm, d = x.shape
    ms = jnp.mean(x * x, axis=1, keepdims=True)
    xn = x * lax.rsqrt(ms + EPS) * g
    h3 = xn.reshape(tm // grp, grp, d) * (1.0 + sc) + sh
    return h3.reshape(tm, d)


def _head_norm(z, gain, dh):
    outs = []
    for hh in range(z.shape[1] // dh):
        zz = z[:, hh * dh:(hh + 1) * dh]
        ms = jnp.mean(zz * zz, axis=1, keepdims=True)
        outs.append(zz * lax.rsqrt(ms + EPS) * gain[:, hh * dh:(hh + 1) * dh])
    return outs


_SEG_OUTS = {"hn": 1, "hn_split": 3, "split": 3, "bf16": 1, "sig": 1}


def _proj_kernel(*refs, segs, grp, dh, npt, with_small, n_alias):
    x_ref, g_ref, sc_ref, sh_ref, w_ref, gain_ref = refs[:6]
    pos = 6
    if with_small:
        ws_ref, bs_ref = refs[6:8]
        pos = 8
    pos += n_alias
    h_scr = refs[-1]
    out_refs = refs[pos:-1]
    i = pl.program_id(0)
    j = pl.program_id(1)

    @pl.when(j == 0)
    def _():
        h = _norm_mod(x_ref[...], g_ref[...], sc_ref[...], sh_ref[...], grp).astype(BF16)
        h_scr[...] = h
        if with_small:
            ga_ref, gb_ref = out_refs[-2:]
            zs = _dot(h, ws_ref[...]) + bs_ref[...]
            lane = lax.broadcasted_iota(jnp.int32, (1, LANES), 1)
            ga_ref[...] = jnp.where(lane < with_small, _log_sigmoid(zs[:, :LANES]), 0.0)
            gb_ref[...] = zs[:, LANES:]

    tm = h_scr.shape[0]
    tn = w_ref.shape[1]
    mb = min(tm, PROJ_ROW_BATCH)
    o = 0
    j0 = 0
    for kind, nt in segs:
        outs = out_refs[o:o + _SEG_OUTS[kind]]
        o += _SEG_OUTS[kind]

        def seg_rows(rows, kind, outs):
            z = _dot(h_scr[rows, :], w_ref[...])
            if kind in ("hn", "hn_split"):
                parts = _head_norm(z, gain_ref[...], dh)
            elif kind == "sig":
                parts = [_sigmoid(z)]
            else:
                parts = [z]
            w = tn // len(parts)
            if kind in ("hn_split", "split"):
                p_ref, s_ref, b_ref = outs
                for k, part in enumerate(parts):
                    b_ref[rows, k * w:(k + 1) * w] = part.astype(BF16)

                @pl.when(i < npt)
                def _():
                    for k, part in enumerate(parts):
                        p_ref[rows, k * w:(k + 1) * w] = part

                @pl.when(i >= npt)
                def _():
                    for k, part in enumerate(parts):
                        s_ref[rows, k * w:(k + 1) * w] = part
            else:
                for k, part in enumerate(parts):
                    outs[0][rows, k * w:(k + 1) * w] = part.astype(BF16)

        def seg_body(kind=kind, outs=outs):
            for rb in range(tm // mb):
                seg_rows(slice(rb * mb, (rb + 1) * mb), kind, outs)

        pl.when((j >= j0) & (j < j0 + nt))(seg_body)
        j0 += nt


def _proj_call(x, g, sc, sh, w, gain, segs, *, tm, tn, grp, dh, n_prompt, small=None, name,
               layer=0, depth=1, stacks=None):
    n, d = x.shape
    nt_total = w.shape[1] // tn
    npt = n_prompt // tm
    nst = (n - n_prompt) // tm
    gr = tm // grp

    def clipj(j, j0, nt):
        return jnp.clip(j - j0, 0, nt - 1)

    in_specs = [pl.BlockSpec((tm, d), lambda i, j: (i, 0)),
                pl.BlockSpec((1, d), lambda i, j: (0, 0)),
                pl.BlockSpec((gr, 1, d), lambda i, j: (i, 0, 0)),
                pl.BlockSpec((gr, 1, d), lambda i, j: (i, 0, 0)),
                pl.BlockSpec((None, d, tn), lambda i, j: (j, 0, 0)),
                pl.BlockSpec((1, tn), lambda i, j: (0, j))]
    args = [x, g, sc, sh, w.reshape(d, nt_total, tn).transpose(1, 0, 2), gain]
    n_gate = 0
    if small is not None:
        ws, bs, n_gate = small
        in_specs += [pl.BlockSpec(ws.shape, lambda i, j: (0, 0)),
                     pl.BlockSpec(bs.shape, lambda i, j: (0, 0))]
        args += [ws, bs]

    out_specs, out_shapes, stack_outs = [], [], []
    j0 = 0
    for kind, nt in segs:
        width = nt * tn
        all_spec = pl.BlockSpec((tm, tn), functools.partial(
            lambda i, j, j0, nt: (i, clipj(j, j0, nt)), j0=j0, nt=nt))
        if kind in ("hn_split", "split"):
            stack_outs.append(len(out_specs))
            out_specs.append(pl.BlockSpec((None, tm, tn), functools.partial(
                lambda i, j, j0, nt: (layer, jnp.minimum(i, npt - 1),
                                      jnp.where(i < npt, clipj(j, j0, nt), nt - 1)), j0=j0, nt=nt)))
            out_shapes.append(jax.ShapeDtypeStruct((depth, n_prompt, width), F32))
            stack_outs.append(len(out_specs))
            out_specs.append(pl.BlockSpec((None, tm, tn), functools.partial(
                lambda i, j, j0, nt: (layer, jnp.maximum(i - npt, 0),
                                      jnp.where(i >= npt, clipj(j, j0, nt), 0)), j0=j0, nt=nt)))
            out_shapes.append(jax.ShapeDtypeStruct((depth, nst * tm, width), F32))
        out_specs.append(all_spec)
        out_shapes.append(jax.ShapeDtypeStruct((n, width), BF16))
        j0 += nt
    assert j0 == nt_total
    if small is not None:
        for _ in range(2):
            out_specs.append(pl.BlockSpec((tm, LANES), lambda i, j: (i, 0)))
            out_shapes.append(jax.ShapeDtypeStruct((n, LANES), F32))
    aliases = {}
    if stacks is not None:
        assert len(stacks) == len(stack_outs)
        for arr, o in zip(stacks, stack_outs):
            aliases[len(args)] = o
            in_specs.append(pl.BlockSpec(memory_space=pl.ANY))
            args.append(arr)

    return pl.pallas_call(
        functools.partial(_proj_kernel, segs=tuple(segs), grp=grp, dh=dh, npt=npt, with_small=n_gate,
                          n_alias=len(aliases)),
        grid=(n // tm, nt_total),
        in_specs=in_specs,
        out_specs=out_specs,
        out_shape=out_shapes,
        scratch_shapes=[pltpu.VMEM((tm, d), BF16)],
        input_output_aliases=aliases,
        compiler_params=_params("arbitrary", "arbitrary"),
        name=name,
    )(*args)


def _scan_kernel(ga_ref, gb_ref, init_ref, lcs_ref, bcum_ref, b_ref, carry):
    c = pl.program_id(1)

    @pl.when(c == 0)
    def _():
        carry[...] = init_ref[...]

    a = ga_ref[...]
    ln = a.shape[0]
    r = lax.broadcasted_iota(jnp.int32, (ln, ln), 0)
    cc = lax.broadcasted_iota(jnp.int32, (ln, ln), 1)
    tri = jnp.where(r >= cc, 1.0, 0.0).astype(BF16)
    hi, mid, lo = _split3(a)
    cs = _dot(tri, hi) + _dot(tri, mid) + _dot(tri, lo)
    lcs_ref[...] = cs + carry[...]
    bcum_ref[...] = cs
    b_ref[...] = gb_ref[...] - cs
    carry[...] = carry[...] + cs[ln - 1:ln, :]


def _scan_call(ga, gb, init, chunk, name):
    b, t, _ = ga.shape
    spec = pl.BlockSpec((None, chunk, LANES), lambda bi, c: (bi, c, 0))
    shp = jax.ShapeDtypeStruct((b, t, LANES), F32)
    return pl.pallas_call(
        _scan_kernel,
        grid=(b, t // chunk),
        in_specs=[spec, spec, pl.BlockSpec((None, 1, LANES), lambda bi, c: (bi, 0, 0))],
        out_specs=[spec, spec, spec],
        out_shape=[shp, shp, shp],
        scratch_shapes=[pltpu.VMEM((1, LANES), F32)],
        compiler_params=_params("arbitrary", "arbitrary"),
        name=name,
    )(ga, gb, init)


def _bias_lanes(x_row, own):
    x = jnp.broadcast_to(x_row, (LANES, x_row.shape[1])).T
    hi, mid, lo = _split3(x)
    lane = lax.broadcasted_iota(jnp.int32, x.shape, 1)
    base = 0 if own else 3
    const = jnp.where((lane >= 3 - base) & (lane < 6 - base), -1.0 if own else 1.0, 0.0).astype(BF16)
    return jnp.where(lane == base, hi, jnp.where(lane == base + 1, mid, jnp.where(lane == base + 2, lo, const)))


def _fox_kernel(q_ref, k_ref, v_ref, lrow_ref, o_ref, vt_scr, ka_scr, qa_scr, m_scr, l_scr, acc_scr,
                sa_scr, ca_scr, sb_scr, cb_scr, *, tb, nh, dh):
    qi = pl.program_id(2)
    nblk = k_ref.shape[0] // tb

    @pl.when(qi == 0)
    def _():
        def prep(c, carry):
            off = pl.multiple_of(c * tb, tb)
            for hh in range(nh):
                v = v_ref[pl.ds(off, tb), hh * dh:(hh + 1) * dh]
                vt_scr[hh, c] = v.astype(F32).T.astype(BF16)
                ka_scr[hh, c, :, :dh] = k_ref[pl.ds(off, tb), hh * dh:(hh + 1) * dh]
                ka_scr[hh, c, :, dh:] = _bias_lanes(lrow_ref[hh, c], False)
            return carry

        lax.fori_loop(0, nblk, prep, 0)

    for hh in range(nh):
        m_scr[hh] = jnp.full((1, tb), NEG_INF, F32)
        l_scr[hh] = jnp.zeros((1, tb), F32)
        acc_scr[hh] = jnp.zeros((dh, tb), F32)
        qa_scr[hh, :, :dh] = q_ref[:, hh * dh:(hh + 1) * dh]
        qa_scr[hh, :, dh:] = _bias_lanes(lrow_ref[hh, qi], True)

    def scores(ki, s_buf, c_buf, masked):
        for hh in range(nh):
            st = _dot_nt(ka_scr[hh, ki], qa_scr[hh])
            if masked:
                r = lax.broadcasted_iota(jnp.int32, (tb, tb), 0)
                c = lax.broadcasted_iota(jnp.int32, (tb, tb), 1)
                st = jnp.where(r <= c, st, NEG_INF)
            s_buf[hh] = st
            c_buf[hh] = jnp.max(st, axis=0, keepdims=True)

    def accumulate(ki, s_buf, c_buf):
        for hh in range(nh):
            m_prev = m_scr[hh]
            m_new = jnp.maximum(m_prev, c_buf[hh])
            p = jnp.exp2(s_buf[hh] - m_new)
            alpha = jnp.exp2(m_prev - m_new)
            l_scr[hh] = alpha * l_scr[hh] + jnp.sum(p, axis=0, keepdims=True)
            acc_scr[hh] = alpha * acc_scr[hh] + _dot(vt_scr[hh, ki], p.astype(BF16))
            m_scr[hh] = m_new

    scores(qi, sa_scr, ca_scr, True)

    def pair(tstep):
        k1 = qi - 1 - 2 * tstep
        scores(k1, sb_scr, cb_scr, False)
        accumulate(k1 + 1, sa_scr, ca_scr)
        scores(k1 - 1, sa_scr, ca_scr, False)
        accumulate(k1, sb_scr, cb_scr)

    def quad(u, carry):
        pair(2 * u)
        pair(2 * u + 1)
        return carry

    n_pairs = qi // 2
    lax.fori_loop(0, n_pairs // 2, quad, 0)

    @pl.when(n_pairs % 2 == 1)
    def _():
        pair(n_pairs - 1)

    @pl.when(qi % 2 == 1)
    def _():
        scores(0, sb_scr, cb_scr, False)
        accumulate(1, sa_scr, ca_scr)
        accumulate(0, sb_scr, cb_scr)

    @pl.when(qi % 2 == 0)
    def _():
        accumulate(0, sa_scr, ca_scr)

    for hh in range(nh):
        o = acc_scr[hh] * (1.0 / l_scr[hh])
        o_ref[:, hh * dh:(hh + 1) * dh] = o.T.astype(o_ref.dtype)


def _fox_prompt_call(q, k, v, lrows, *, b, t, fh, dh, tb, nh):
    n = q.shape[0]
    nq = t // tb
    hw = nh * dh
    return pl.pallas_call(
        functools.partial(_fox_kernel, tb=tb, nh=nh, dh=dh),
        grid=(b, fh // nh, nq),
        in_specs=[pl.BlockSpec((tb, hw), lambda bi, h, qi: (bi * nq + qi, h)),
                  pl.BlockSpec((t, hw), lambda bi, h, qi: (bi, h)),
                  pl.BlockSpec((t, hw), lambda bi, h, qi: (bi, h)),
                  pl.BlockSpec((nh, nq, 1, tb), lambda bi, h, qi: (bi * (fh // nh) + h, 0, 0, 0))],
        out_specs=pl.BlockSpec((tb, hw), lambda bi, h, qi: (bi * nq + qi, h)),
        out_shape=jax.ShapeDtypeStruct((n, fh * dh), BF16),
        scratch_shapes=[pltpu.VMEM((nh, nq, dh, tb), BF16), pltpu.VMEM((nh, nq, tb, dh + LANES), BF16),
                        pltpu.VMEM((nh, tb, dh + LANES), BF16),
                        pltpu.VMEM((nh, 1, tb), F32), pltpu.VMEM((nh, 1, tb), F32),
                        pltpu.VMEM((nh, dh, tb), F32),
                        pltpu.VMEM((nh, tb, tb), F32), pltpu.VMEM((nh, 1, tb), F32),
                        pltpu.VMEM((nh, tb, tb), F32), pltpu.VMEM((nh, 1, tb), F32)],
        compiler_params=_params("arbitrary", "arbitrary", "arbitrary"),
        name="fox_prompt",
    )(q, k, v, lrows)


def _fox_sample_kernel(q_ref, kp_ref, vp_ref, kn_ref, vn_ref, lcs_ref, lkp_ref, lkn_ref, y_any, o_ref, *, fh, dh):
    del y_any
    ts = q_ref.shape[0]
    lcs = lcs_ref[...]
    r = lax.broadcasted_iota(jnp.int32, (ts, ts), 0)
    c = lax.broadcasted_iota(jnp.int32, (ts, ts), 1)
    for h in range(fh):
        cols = slice(h * dh, (h + 1) * dh)
        q = q_ref[:, cols]
        lq = _lane_select(lcs, h)[:, :1] * LOG2E
        sp = _dot_nt(q, kp_ref[:, h, :].astype(BF16)) + lq - lkp_ref[h:h + 1, :]
        sn = _dot_nt(q, kn_ref[:, cols]) + lq - lkn_ref[h:h + 1, :]
        sn = jnp.where(r >= c, sn, NEG_INF)
        m = jnp.maximum(jnp.max(sp, axis=1, keepdims=True), jnp.max(sn, axis=1, keepdims=True))
        pp = jnp.exp2(sp - m)
        pn = jnp.exp2(sn - m)
        den = jnp.sum(pp, axis=1, keepdims=True) + jnp.sum(pn, axis=1, keepdims=True)
        acc = _dot(pp.astype(BF16), vp_ref[:, h, :].astype(BF16)) + _dot(pn.astype(BF16), vn_ref[:, cols])
        o_ref[:, cols] = (acc / den).astype(o_ref.dtype)


def _fox_sample_call(y, q, kb, vb, cache_k, cache_v, lcs_s, lkp_rows, lkn_rows, *, layer, n_prompt, bs, ts, fh, dh):
    p = cache_k.shape[2]
    r0 = n_prompt // ts
    fw = fh * dh
    return pl.pallas_call(
        functools.partial(_fox_sample_kernel, fh=fh, dh=dh),
        grid=(bs,),
        in_specs=[pl.BlockSpec((ts, fw), lambda bi: (r0 + bi, 0)),
                  pl.BlockSpec((None, None, p, fh, dh), lambda bi: (layer, bi, 0, 0, 0)),
                  pl.BlockSpec((None, None, p, fh, dh), lambda bi: (layer, bi, 0, 0, 0)),
                  pl.BlockSpec((ts, fw), lambda bi: (r0 + bi, 0)),
                  pl.BlockSpec((ts, fw), lambda bi: (r0 + bi, 0)),
                  pl.BlockSpec((ts, LANES), lambda bi: (bi, 0)),
                  pl.BlockSpec((None, fh, p), lambda bi: (bi, 0, 0)),
                  pl.BlockSpec((None, fh, ts), lambda bi: (bi, 0, 0)),
                  pl.BlockSpec(memory_space=pl.ANY)],
        out_specs=pl.BlockSpec((ts, fw), lambda bi: (r0 + bi, 0)),
        out_shape=jax.ShapeDtypeStruct(y.shape, y.dtype),
        input_output_aliases={8: 0},
        compiler_params=_params("arbitrary"),
        name="fox_sample",
    )(q, cache_k, cache_v, kb, vb, lcs_s, lkp_rows, lkn_rows, y)


def _mlstm_kernel(*refs, hoff, aliased):
    if aliased:
        refs = refs[:11] + refs[12:]
    (q_ref, k_ref, v_ref, o_ref, bcum_ref, bnat_ref, brow_ref, g_ref, c0_ref, n0_ref, m0_ref,
     y_ref, c1_ref, n1_ref, m1_ref, c_scr, n_scr, m_scr) = refs
    h = pl.program_id(1)
    c = pl.program_id(2)
    nc = pl.num_programs(2)

    @pl.when(c == 0)
    def _():
        c_scr[...] = c0_ref[...]
        n_scr[...] = n0_ref[...]
        m_scr[...] = m0_ref[...]

    q = q_ref[...]
    k = k_ref[...]
    v = v_ref[...]
    ln = q.shape[0]
    bcum = _lane_select(bcum_ref[...], hoff + h)[:, :1]
    bcol = _lane_select(bnat_ref[...], hoff + h)[:, :1]
    brow = brow_ref[...]
    m0 = m_scr[:, :1]
    r = lax.broadcasted_iota(jnp.int32, (ln, ln), 0)
    cc = lax.broadcasted_iota(jnp.int32, (ln, ln), 1)
    bm = jnp.where(r >= cc, brow, NEG_INF)
    g = jnp.maximum(m0, jnp.max(bm, axis=1, keepdims=True))
    dmat = jnp.exp(bm - g)
    s = _dot_nt(q, k) * dmat
    w_inter = jnp.exp(m0 - g)
    cmat = c_scr[...]
    num = _dot(s.astype(BF16), v) + _dot(q, cmat.astype(BF16)) * w_inter
    qn = jnp.sum(q.astype(F32) * n_scr[...], axis=1, keepdims=True)
    den = jnp.sum(s, axis=1, keepdims=True) + w_inter * qn
    den = jnp.maximum(jnp.abs(den), jnp.exp(-(bcum + g)))
    hval = num * (1.0 / den)
    ms = jnp.mean(hval * hval, axis=1, keepdims=True)
    y = hval * lax.rsqrt(ms + EPS) * g_ref[...] * o_ref[...].astype(F32)
    y_ref[...] = y.astype(y_ref.dtype)

    g_last = g[ln - 1:ln, :]
    w_key = jnp.exp(bcol - g_last)
    f_tot = jnp.exp(m0 - g_last)
    kw = k.astype(F32) * w_key
    c_new = f_tot * cmat + _dot(kw.T.astype(BF16), v)
    n_new = f_tot * n_scr[...] + jnp.sum(kw, axis=0, keepdims=True)
    m_new = jnp.broadcast_to(bcum[ln - 1:ln, :] + g_last, m_scr.shape)
    c_scr[...] = c_new
    n_scr[...] = n_new
    m_scr[...] = m_new

    @pl.when(c == nc - 1)
    def _():
        c1_ref[...] = c_new
        n1_ref[...] = n_new
        m1_ref[...] = m_new


def _mlstm_call(mq, mk, mv, mo, bcum, bnat, brows, gain, c0, n0, m0, *, row0, nb, nc, chunk, mh, dqk, dv,
                hoff, y_prev, name):
    n = mq.shape[0]
    rb0 = row0 // chunk

    def rows(bi, h, c):
        return rb0 + bi * nc + c

    in_specs = [pl.BlockSpec((chunk, dqk), lambda bi, h, c: (rows(bi, h, c), h)),
                pl.BlockSpec((chunk, dqk), lambda bi, h, c: (rows(bi, h, c), h)),
                pl.BlockSpec((chunk, dv), lambda bi, h, c: (rows(bi, h, c), h)),
                pl.BlockSpec((chunk, dv), lambda bi, h, c: (rows(bi, h, c), h)),
                pl.BlockSpec((chunk, LANES), lambda bi, h, c: (bi * nc + c, 0)),
                pl.BlockSpec((chunk, LANES), lambda bi, h, c: (bi * nc + c, 0)),
                pl.BlockSpec((None, None, 1, chunk), lambda bi, h, c: (bi * mh + h, c, 0, 0)),
                pl.BlockSpec((None, 1, dv), lambda bi, h, c: (h, 0, 0)),
                pl.BlockSpec((None, None, dqk, dv), lambda bi, h, c: (bi, h, 0, 0)),
                pl.BlockSpec((None, None, 1, dqk), lambda bi, h, c: (bi, h, 0, 0)),
                pl.BlockSpec((None, None, 1, LANES), lambda bi, h, c: (bi, h, 0, 0))]
    args = [mq, mk, mv, mo, bcum, bnat, brows, gain, c0, n0, m0]
    aliases = {}
    if y_prev is not None:
        in_specs.append(pl.BlockSpec(memory_space=pl.ANY))
        args.append(y_prev)
        aliases = {len(args) - 1: 0}
    return pl.pallas_call(
        functools.partial(_mlstm_kernel, hoff=hoff, aliased=y_prev is not None),
        grid=(nb, mh, nc),
        in_specs=in_specs,
        out_specs=[pl.BlockSpec((chunk, dv), lambda bi, h, c: (rows(bi, h, c), h)),
                   pl.BlockSpec((None, None, dqk, dv), lambda bi, h, c: (bi, h, 0, 0)),
                   pl.BlockSpec((None, None, 1, dqk), lambda bi, h, c: (bi, h, 0, 0)),
                   pl.BlockSpec((None, None, 1, LANES), lambda bi, h, c: (bi, h, 0, 0))],
        out_shape=[jax.ShapeDtypeStruct((n, mh * dv), BF16),
                   jax.ShapeDtypeStruct((nb, mh, dqk, dv), F32),
                   jax.ShapeDtypeStruct((nb, mh, 1, dqk), F32),
                   jax.ShapeDtypeStruct((nb, mh, 1, LANES), F32)],
        scratch_shapes=[pltpu.VMEM((dqk, dv), F32), pltpu.VMEM((1, dqk), F32), pltpu.VMEM((1, LANES), F32)],
        input_output_aliases=aliases,
        compiler_params=_params("arbitrary", "arbitrary", "arbitrary"),
        name=name,
    )(*args)


def _route(logits, b_router, n_exp, epg):
    n_grp = n_exp // epg
    lane = lax.broadcasted_iota(jnp.int32, logits.shape, 1)
    valid = lane < n_exp
    pos = lane % epg
    grp = lane // epg
    s = _sigmoid(logits)
    sb = jnp.where(valid, s + b_router, NEG_INF)

    def nbr(x, d):
        return pltpu.roll(x, (-d) % LANES, axis=1)

    def count_beaten(x, idx, span, step):
        cnt = jnp.zeros(x.shape, F32)
        for d in range(-(span - 1), span):
            if d == 0:
                continue
            other = nbr(x, d * step)
            inside = (idx + d >= 0) & (idx + d < span)
            ahead = (other > x) | ((other == x) & (d < 0))
            cnt = cnt + jnp.where(inside & ahead, 1.0, 0.0)
        return cnt

    top2 = count_beaten(sb, pos, epg, 1) < 2.0
    t2v = jnp.where(top2, sb, 0.0)
    t2v = jnp.where(valid, t2v, NEG_INF)
    gscore = t2v
    for d in range(-(epg - 1), epg):
        if d == 0:
            continue
        inside = (pos + d >= 0) & (pos + d < epg)
        gscore = gscore + jnp.where(inside, nbr(t2v, d), 0.0)
    best_grp = count_beaten(gscore, grp, n_grp, epg) < 1.0
    sel = valid & best_grp & top2
    w = jnp.where(sel, s, 0.0)
    return w / jnp.sum(w, axis=1, keepdims=True), jnp.where(sel, 1.0, 0.0)


def _merge_kernel(x_ref, g1n_ref, sc1_ref, sh1_ref, gt1_ref, yf_ref, ym_ref, wga_ref, wgb_ref, bga_ref, bgb_ref,
                  wo_ref, g2n_ref, sc2_ref, sh2_ref, wrh_ref, wrl_ref, br_ref,
                  x1_ref, h2_ref, gates_ref, sel_ref, h_scr, acc_scr, *, grp, n_exp, epg):
    j = pl.program_id(1)
    nj = pl.num_programs(1)

    @pl.when(j == 0)
    def _():
        h_scr[...] = _norm_mod(x_ref[...], g1n_ref[...], sc1_ref[...], sh1_ref[...], grp).astype(BF16)
        acc_scr[...] = jnp.zeros(acc_scr.shape, F32)

    mb = min(h_scr.shape[0], PROJ_ROW_BATCH)
    for rb in range(h_scr.shape[0] // mb):
        rows = slice(rb * mb, (rb + 1) * mb)
        h = h_scr[rows, :]
        ga = _dot(h, wga_ref[...]) + bga_ref[...]
        gb = _dot(h, wgb_ref[...]) + bgb_ref[...]
        mix = _sigmoid(ga) * yf_ref[rows, :].astype(F32) + _sigmoid(gb) * ym_ref[rows, :].astype(F32)
        acc_scr[rows, :] += _dot(mix.astype(BF16), wo_ref[...])

    @pl.when(j == nj - 1)
    def _():
        tm, d = x_ref.shape
        acc3 = acc_scr[...].reshape(tm // grp, grp, d) * gt1_ref[...]
        x1 = x_ref[...] + acc3.reshape(tm, d)
        x1_ref[...] = x1
        h2 = _norm_mod(x1, g2n_ref[...], sc2_ref[...], sh2_ref[...], grp)
        hi = h2.astype(BF16)
        lo = (h2 - hi.astype(F32)).astype(BF16)
        h2_ref[...] = h2
        logits = _dot(hi, wrh_ref[...]) + _dot(lo, wrh_ref[...]) + _dot(hi, wrl_ref[...])
        gates_ref[...], sel_ref[...] = _route(logits, br_ref[...], n_exp, epg)


def _merge_call(x, g1n, sc1, sh1, gt1, yf, ym, wgate, bgate, wout, g2n, sc2, sh2, wr_hi, wr_lo, br,
                *, tm, tc, grp, n_exp, epg):
    n, d = x.shape
    nj = d // tc
    gr = tm // grp
    wg3 = wgate.reshape(d, 2 * nj, tc).transpose(1, 0, 2)
    row = pl.BlockSpec((tm, d), lambda i, j: (i, 0))
    vec = pl.BlockSpec((1, d), lambda i, j: (0, 0))
    mod = pl.BlockSpec((gr, 1, d), lambda i, j: (i, 0, 0))
    return pl.pallas_call(
        functools.partial(_merge_kernel, grp=grp, n_exp=n_exp, epg=epg),
        grid=(n // tm, nj),
        in_specs=[row, vec, mod, mod, mod,
                  pl.BlockSpec((tm, tc), lambda i, j: (i, j)),
                  pl.BlockSpec((tm, tc), lambda i, j: (i, j)),
                  pl.BlockSpec((None, d, tc), lambda i, j: (j, 0, 0)),
                  pl.BlockSpec((None, d, tc), lambda i, j: (nj + j, 0, 0)),
                  pl.BlockSpec((1, tc), lambda i, j: (0, j)),
                  pl.BlockSpec((1, tc), lambda i, j: (0, nj + j)),
                  pl.BlockSpec((tc, d), lambda i, j: (j, 0)),
                  vec, mod, mod,
                  pl.BlockSpec((d, LANES), lambda i, j: (0, 0)),
                  pl.BlockSpec((d, LANES), lambda i, j: (0, 0)),
                  pl.BlockSpec((1, LANES), lambda i, j: (0, 0))],
        out_specs=[row, row, pl.BlockSpec((tm, LANES), lambda i, j: (i, 0)),
                   pl.BlockSpec((tm, LANES), lambda i, j: (i, 0))],
        out_shape=[jax.ShapeDtypeStruct((n, d), F32), jax.ShapeDtypeStruct((n, d), F32),
                   jax.ShapeDtypeStruct((n, LANES), F32), jax.ShapeDtypeStruct((n, LANES), F32)],
        scratch_shapes=[pltpu.VMEM((tm, d), BF16), pltpu.VMEM((tm, d), F32)],
        compiler_params=_params("arbitrary", "arbitrary"),
        name="merge_out",
    )(x, g1n, sc1, sh1, gt1, yf, ym, wg3, wg3, bgate, bgate, wout, g2n, sc2, sh2, wr_hi, wr_lo, br)


def _slot_table_kernel(p0_ref, p1_ref, lo_ref, hi_ref, idx_ref, *, n_tok):
    for k in range(lo_ref.shape[0]):
        def clear(r, c):
            idx_ref[r] = 0
            return c

        lax.fori_loop(lo_ref[k], hi_ref[k], clear, 0)

    def fill(r, c):
        idx_ref[p0_ref[r]] = r
        idx_ref[p1_ref[r]] = r
        return c

    lax.fori_loop(0, n_tok, fill, 0, unroll=8)


def _slot_table_call(pos0, pos1, pad_lo, pad_hi, n_slot):
    n_tok = pos0.shape[0]
    smem = pl.BlockSpec(memory_space=pltpu.SMEM)
    return pl.pallas_call(
        functools.partial(_slot_table_kernel, n_tok=n_tok),
        in_specs=[smem, smem, smem, smem],
        out_specs=smem,
        out_shape=jax.ShapeDtypeStruct((n_slot,), jnp.int32),
        name="moe_slots",
    )(pos0, pos1, pad_lo, pad_hi)


def _gather_rows(src_any, idx_ref, base, dst, sem, n_rows, inline=False):
    def issue(r, c):
        pltpu.make_async_copy(src_any.at[pl.ds(idx_ref[base + r], 1)], dst.at[pl.ds(r, 1)], sem).start()
        return c

    if inline:
        for r in range(n_rows):
            issue(r, 0)
    else:
        lax.fori_loop(0, n_rows, issue, 0, unroll=8)


def _wait_rows(src_any, dst, sem):
    pltpu.make_async_copy(src_any.at[pl.ds(0, dst.shape[0])], dst, sem).wait()


def _expert_kernel(idx_ref, te_ref, nu_ref, h_any, wg_ref, wu_ref, wd_ref, ys_ref,
                   xbuf, wgb, wub, wdb, sem, *, tmx):
    t = pl.program_id(0)
    nt = pl.num_programs(0)
    ahead = GATHER_SLOTS - 1
    slot = t % GATHER_SLOTS

    @pl.when(t == 0)
    def _():
        for k in range(ahead):
            _gather_rows(h_any, idx_ref, jnp.minimum(k, nt - 1) * tmx, xbuf.at[k], sem.at[k], tmx)

    @pl.when((t == 0) | (te_ref[t] != te_ref[jnp.maximum(t - 1, 0)]))
    def _():
        wgb[...] = wg_ref[...].astype(BF16)
        wub[...] = wu_ref[...].astype(BF16)
        wdb[...] = wd_ref[...].astype(BF16)

    _wait_rows(h_any, xbuf.at[slot], sem.at[slot])
    nxt = jnp.minimum(t + ahead, nt - 1) * tmx
    nslot = (t + ahead) % GATHER_SLOTS

    @pl.when(t < nu_ref[0])
    def _():
        x = xbuf[slot].astype(BF16)
        a = _dot(x, wgb[...])
        u = _dot(x, wub[...])
        y = (a * _sigmoid(a)) * u
        ys_ref[...] = _dot(y.astype(BF16), wdb[...])
        _gather_rows(h_any, idx_ref, nxt, xbuf.at[nslot], sem.at[nslot], tmx, inline=True)

    @pl.when(t >= nu_ref[0])
    def _():
        ys_ref[...] = jnp.zeros(ys_ref.shape, F32)
        _gather_rows(h_any, idx_ref, nxt, xbuf.at[nslot], sem.at[nslot], tmx)

    @pl.when(t == nt - 1)
    def _():
        for k in range(1, GATHER_SLOTS):
            s = (t + k) % GATHER_SLOTS
            _wait_rows(h_any, xbuf.at[s], sem.at[s])


def _expert_call(idx, tile_expert, n_used, h2, wg, wu, wd, *, tmx, layer):
    n, d = h2.shape
    de = wg.shape[-1]
    nt = tile_expert.shape[0]
    grid_spec = pltpu.PrefetchScalarGridSpec(
        num_scalar_prefetch=3,
        grid=(nt,),
        in_specs=[pl.BlockSpec(memory_space=pl.ANY),
                  pl.BlockSpec((None, None, d, de), lambda t, idx, te, nu: (layer, te[t], 0, 0)),
                  pl.BlockSpec((None, None, d, de), lambda t, idx, te, nu: (layer, te[t], 0, 0)),
                  pl.BlockSpec((None, None, de, d), lambda t, idx, te, nu: (layer, te[t], 0, 0))],
        out_specs=pl.BlockSpec((tmx, d), lambda t, idx, te, nu: (t, 0)),
        scratch_shapes=[pltpu.VMEM((GATHER_SLOTS, tmx, d), F32), pltpu.VMEM((d, de), BF16),
                        pltpu.VMEM((d, de), BF16), pltpu.VMEM((de, d), BF16),
                        pltpu.SemaphoreType.DMA((GATHER_SLOTS,))])
    return pl.pallas_call(
        functools.partial(_expert_kernel, tmx=tmx),
        grid_spec=grid_spec,
        out_shape=jax.ShapeDtypeStruct((nt * tmx, d), F32),
        compiler_params=_params("arbitrary"),
        name="moe_experts",
    )(idx, tile_expert, n_used, h2, wg, wu, wd)


def _combine_kernel(p0_ref, p1_ref, ys_any, x1_ref, gt2_ref, wp_ref, *refs, tmc, grp, npt):
    out_refs, (buf, sem) = refs[:-2], refs[-2:]
    i = pl.program_id(0)
    nt = pl.num_programs(0)
    slot = i % 2

    def gather(base, s, inline):
        _gather_rows(ys_any, p0_ref, base, buf.at[s, 0], sem.at[s], tmc, inline)
        _gather_rows(ys_any, p1_ref, base, buf.at[s, 1], sem.at[s], tmc, inline)

    def wait(s):
        _wait_rows(ys_any, buf.at[s, 0], sem.at[s])
        _wait_rows(ys_any, buf.at[s, 1], sem.at[s])

    @pl.when(i == 0)
    def _():
        gather(0, 0, False)

    wait(slot)
    gather(jnp.minimum(i + 1, nt - 1) * tmc, 1 - slot, True)
    w = wp_ref[...]
    moe = w[:, 0:1] * buf[slot, 0] + w[:, 1:2] * buf[slot, 1]
    tm, d = moe.shape
    out = x1_ref[...] + (moe.reshape(tm // grp, grp, d) * gt2_ref[...]).reshape(tm, d)
    if len(out_refs) == 1:
        out_refs[0][...] = out
    else:
        @pl.when(i < npt)
        def _():
            out_refs[0][...] = out

        @pl.when(i >= npt)
        def _():
            out_refs[1][...] = out

    @pl.when(i == nt - 1)
    def _():
        wait(1 - slot)


def _combine_call(pos0, pos1, ys, x1, gt2, wpair, *, tmc, grp, n_prompt, split):
    n, d = x1.shape
    gr = tmc // grp
    npt = n_prompt // tmc
    row = lambda i, p0, p1: (i, 0)
    if split:
        out_specs = [pl.BlockSpec((tmc, d), lambda i, p0, p1: (jnp.minimum(i, npt - 1), 0)),
                     pl.BlockSpec((tmc, d), lambda i, p0, p1: (jnp.maximum(i - npt, 0), 0))]
        out_shape = [jax.ShapeDtypeStruct((n_prompt, d), F32), jax.ShapeDtypeStruct((n - n_prompt, d), F32)]
    else:
        out_specs = [pl.BlockSpec((tmc, d), row)]
        out_shape = [jax.ShapeDtypeStruct((n, d), F32)]
    grid_spec = pltpu.PrefetchScalarGridSpec(
        num_scalar_prefetch=2,
        grid=(n // tmc,),
        in_specs=[pl.BlockSpec(memory_space=pl.ANY),
                  pl.BlockSpec((tmc, d), row),
                  pl.BlockSpec((gr, 1, d), lambda i, p0, p1: (i, 0, 0)),
                  pl.BlockSpec((tmc, LANES), row)],
        out_specs=out_specs,
        scratch_shapes=[pltpu.VMEM((2, 2, tmc, d), F32), pltpu.SemaphoreType.DMA((2,))])
    return pl.pallas_call(
        functools.partial(_combine_kernel, tmc=tmc, grp=grp, npt=npt),
        grid_spec=grid_spec,
        out_shape=out_shape,
        compiler_params=_params("arbitrary"),
        name="moe_combine",
    )(pos0, pos1, ys, x1, gt2, wpair)


def _moe_sparse(h2, gates, sel, x1, gt2, wg, wu, wd, *, layer, tmx, tmc, grp, n_prompt, split):
    n, d = x1.shape
    n_exp = wg.shape[1]
    nt = -(-(2 * n + n_exp * (tmx - 1)) // tmx)
    n_slot = nt * tmx
    cs = _scan_call(sel.reshape(1, n, LANES), sel.reshape(1, n, LANES), jnp.zeros((1, 1, LANES), F32),
                    _pick([n], (256, 128, 64)), "scan_route")[0].reshape(n, LANES)
    counts = cs[n - 1, :n_exp].astype(jnp.int32)
    padded = (counts + tmx - 1) // tmx * tmx
    ends = jnp.cumsum(padded)
    off = jnp.zeros((LANES,), jnp.int32).at[:n_exp].set(ends - padded)
    dest = off[None, :] + cs.astype(jnp.int32) - 1
    chosen = sel > 0.0
    pos0 = jnp.min(jnp.where(chosen, dest, n_slot), axis=1)
    pos1 = jnp.max(jnp.where(chosen, dest, -1), axis=1)
    w0 = jnp.sum(jnp.where(chosen & (dest == pos0[:, None]), gates, 0.0), axis=1)
    w1 = jnp.sum(jnp.where(chosen & (dest == pos1[:, None]), gates, 0.0), axis=1)
    wpair = jnp.zeros((n, LANES), F32).at[:, 0].set(w0).at[:, 1].set(w1)
    tile_expert = jnp.minimum(jnp.sum(jnp.arange(nt)[:, None] * tmx >= ends[None, :], axis=1), n_exp - 1)
    n_used = (ends[n_exp - 1] // tmx).reshape(1).astype(jnp.int32)
    pad_lo = jnp.concatenate([ends - padded + counts, ends[n_exp - 1:]]).astype(jnp.int32)
    pad_hi = jnp.concatenate([ends, jnp.full((1,), n_slot, jnp.int32)]).astype(jnp.int32)
    idx = _slot_table_call(pos0, pos1, pad_lo, pad_hi, n_slot)
    ys = _expert_call(idx, tile_expert.astype(jnp.int32), n_used, h2, wg, wu, wd, tmx=tmx, layer=layer)
    return _combine_call(pos0, pos1, ys, x1, gt2, wpair, tmc=tmc, grp=grp, n_prompt=n_prompt, split=split)


def kernel(x_prompt, x_sample, cache_fox_k, cache_fox_v, cache_fox_logf, state_mlstm_c, state_mlstm_n, state_mlstm_m, c_prompt, c_sample, norm1_g, norm2_g, w_ada, b_ada, w_in, fox_qn_g, fox_kn_g, fox_fb, ml_ib, ml_fb, ml_out_g, w_gate, b_gate, w_out, w_router, b_router, w_e_gate, w_e_up, w_e_down):
    b, t, d = x_prompt.shape
    bs, ts, _ = x_sample.shape
    depth = w_in.shape[0]
    past = cache_fox_k.shape[2]
    fh = fox_fb.shape[-1]
    dh = d // fh
    fw = fh * dh
    mh = ml_ib.shape[-1]
    dqk, dv = state_mlstm_c.shape[-2:]
    qw, vw = mh * dqk, mh * dv
    n_exp = w_router.shape[-1]
    epg = n_exp // 4
    n_p, n_s = b * t, bs * ts
    n = n_p + n_s
    grp = ts
    tm_a = _pick([n_p, n_s], (1024, 512, 256, 128))
    tm_b = _pick([n_p, n_s], (512, 256, 128))
    tn = _pick([fw, qw, vw], (512, 256, 128))
    tc = _pick([d], (512, 256, 128))
    tb = _pick([t], (512, 256, 128))
    nh = 2 if fh % 2 == 0 else 1
    tmx = 256
    tmc = _pick([n_p, n_s], (256, 128))
    lp = _pick([t], (256, 128, 64))
    pc = _pick([past], (512, 256, 128, 64))
    assert fh + mh <= LANES and n_exp <= LANES and t % grp == 0 and tm_b % grp == 0

    x = jnp.concatenate([x_prompt.reshape(n_p, d), x_sample.reshape(n_s, d)], axis=0)
    n_seq = b + bs
    r_pad = -(-n_seq // 8) * 8
    c_all = jnp.concatenate([c_prompt, c_sample, jnp.zeros((r_pad - n_seq, d), F32)], axis=0)
    mod = _ada_call(c_all, w_ada, b_ada)
    seq_of_group = jnp.concatenate([jnp.repeat(jnp.arange(b), t // grp), b + jnp.arange(bs)])

    cuts = [0]
    for wdt in (fw, fw, fw, fh, qw, qw, vw, mh, mh, vw):
        cuts.append(cuts[-1] + wdt)
    (c_fq, c_fk, c_fv, c_ff, c_mq, c_mk, c_mv, c_mi, c_mf, c_mo, _) = cuts

    wr_hi = jnp.zeros((d, LANES), F32).at[:, :n_exp].set(w_router)
    wr_lo = (wr_hi - wr_hi.astype(BF16).astype(F32)).astype(BF16)
    wr_hi = wr_hi.astype(BF16)
    br = jnp.zeros((1, LANES), F32).at[0, :n_exp].set(b_router)
    zeros_c = jnp.zeros((b, mh, dqk, dv), F32)
    zeros_n = jnp.zeros((b, mh, 1, dqk), F32)
    zeros_m = jnp.zeros((b, mh, 1, LANES), F32)

    outs_p, outs_s = [], []
    kv_stacks = None
    for l in range(depth):
        modl = mod[l][seq_of_group]
        sh1, sc1, gt1, sh2, sc2, gt2 = [modl[:, k * d:(k + 1) * d].reshape(n // grp, 1, d) for k in range(6)]
        wl = w_in[l]
        w1 = wl[:, :c_ff].astype(BF16)
        w2 = jnp.concatenate([wl[:, c_mq:c_mk] * (dqk ** -0.5), wl[:, c_mk:c_mi], wl[:, c_mo:]], axis=1).astype(BF16)
        ws = jnp.zeros((d, 2 * LANES), F32)
        ws = ws.at[:, :fh].set(wl[:, c_ff:c_mq]).at[:, fh:fh + mh].set(wl[:, c_mf:c_mo])
        ws = ws.at[:, LANES + fh:LANES + fh + mh].set(wl[:, c_mi:c_mf]).astype(BF16)
        bsm = jnp.zeros((1, 2 * LANES), F32)
        bsm = bsm.at[0, :fh].set(fox_fb[l]).at[0, fh:fh + mh].set(ml_fb[l])
        bsm = bsm.at[0, LANES + fh:LANES + fh + mh].set(ml_ib[l])
        gain1 = jnp.concatenate([jnp.tile(fox_qn_g[l] * (LOG2E * dh ** -0.5), fh), jnp.tile(fox_kn_g[l], fh),
                                 jnp.ones((fw,), F32)]).reshape(1, 3 * fw)
        g1n = norm1_g[l].reshape(1, d)
        g2n = norm2_g[l].reshape(1, d)

        q_b, k_p, k_s, k_b, v_p, v_s, v_b, g_a, g_b = _proj_call(
            x, g1n, sc1, sh1, w1, gain1, [("hn", fw // tn), ("hn_split", fw // tn), ("split", fw // tn)],
            tm=tm_a, tn=tn, grp=grp, dh=dh, n_prompt=n_p, small=(ws, bsm, fh + mh), name="proj_fox",
            layer=l, depth=depth, stacks=kv_stacks)
        kv_stacks = [k_p, k_s, v_p, v_s]
        m_q, m_k, m_v, m_o = _proj_call(
            x, g1n, sc1, sh1, w2, jnp.ones((1, w2.shape[1]), F32),
            [("bf16", qw // tn), ("bf16", qw // tn), ("bf16", vw // tn), ("sig", vw // tn)],
            tm=tm_a, tn=tn, grp=grp, dh=dh, n_prompt=n_p, name="proj_mlstm")

        lcs_p, bcum_p, bnat_p = _scan_call(g_a[:n_p].reshape(b, t, LANES), g_b[:n_p].reshape(b, t, LANES),
                                           jnp.zeros((b, 1, LANES), F32), lp, "scan_prompt")
        past_lf = jnp.zeros((bs, past, LANES), F32).at[:, :, :fh].set(cache_fox_logf[l])
        lcs_c, _, _ = _scan_call(past_lf, past_lf, jnp.zeros((bs, 1, LANES), F32), pc, "scan_cache")
        lcs_s, bcum_s, bnat_s = _scan_call(g_a[n_p:].reshape(bs, ts, LANES), g_b[n_p:].reshape(bs, ts, LANES),
                                           lcs_c[:, past - 1:past, :], ts, "scan_sample")

        lk_p = (jnp.transpose(lcs_p[:, :, :fh], (0, 2, 1)) * LOG2E).reshape(b * fh, t // tb, 1, tb)
        y_fox = _fox_prompt_call(q_b, k_b, v_b, lk_p, b=b, t=t, fh=fh, dh=dh, tb=tb, nh=nh)
        lk_c = jnp.transpose(lcs_c[:, :, :fh], (0, 2, 1)) * LOG2E
        lk_n = jnp.transpose(lcs_s[:, :, :fh], (0, 2, 1)) * LOG2E
        y_fox = _fox_sample_call(y_fox, q_b, k_b, v_b, cache_fox_k, cache_fox_v, lcs_s.reshape(n_s, LANES),
                                 lk_c, lk_n, layer=l, n_prompt=n_p, bs=bs, ts=ts, fh=fh, dh=dh)

        gain_m = ml_out_g[l].reshape(mh, 1, dv)
        br_p = jnp.transpose(bnat_p[:, :, fh:fh + mh], (0, 2, 1)).reshape(b * mh, t // lp, 1, lp)
        y_ml, c1_p, n1_p, m1_p = _mlstm_call(
            m_q, m_k, m_v, m_o, bcum_p.reshape(n_p, LANES), bnat_p.reshape(n_p, LANES), br_p, gain_m,
            zeros_c, zeros_n, zeros_m, row0=0, nb=b, nc=t // lp, chunk=lp, mh=mh, dqk=dqk, dv=dv, hoff=fh,
            y_prev=None, name="mlstm_prompt")
        br_s = jnp.transpose(bnat_s[:, :, fh:fh + mh], (0, 2, 1)).reshape(bs * mh, 1, 1, ts)
        y_ml, c1_s, n1_s, m1_s = _mlstm_call(
            m_q, m_k, m_v, m_o, bcum_s.reshape(n_s, LANES), bnat_s.reshape(n_s, LANES), br_s, gain_m,
            state_mlstm_c[l], state_mlstm_n[l].reshape(bs, mh, 1, dqk),
            jnp.broadcast_to(state_mlstm_m[l][:, :, None, None], (bs, mh, 1, LANES)),
            row0=n_p, nb=bs, nc=1, chunk=ts, mh=mh, dqk=dqk, dv=dv, hoff=fh, y_prev=y_ml, name="mlstm_sample")

        x1, h2, gates, sel = _merge_call(
            x, g1n, sc1, sh1, gt1, y_fox, y_ml, w_gate[l].astype(BF16), b_gate[l].reshape(1, 2 * d),
            w_out[l].astype(BF16), g2n, sc2, sh2, wr_hi, wr_lo, br, tm=tm_b, tc=tc, grp=grp, n_exp=n_exp, epg=epg)
        moe_out = _moe_sparse(h2, gates, sel, x1, gt2, w_e_gate, w_e_up, w_e_down, layer=l, tmx=tmx, tmc=tmc,
                              grp=grp, n_prompt=n_p, split=l == depth - 1)
        x = moe_out[0]

        outs_p.append((g_a[:n_p, :fh].reshape(b, t, fh), c1_p, n1_p.reshape(b, mh, dqk), m1_p[:, :, 0, 0]))
        outs_s.append((g_a[n_p:, :fh].reshape(bs, ts, fh), c1_s, n1_s.reshape(bs, mh, dqk), m1_s[:, :, 0, 0]))

    def stack(outs, i):
        return jnp.stack([o[i] for o in outs], axis=0)

    k_p, k_s, v_p, v_s = kv_stacks
    return (moe_out[0].reshape(b, t, d), moe_out[1].reshape(bs, ts, d),
            k_p.reshape(depth, b, t, fh, dh), v_p.reshape(depth, b, t, fh, dh),
            stack(outs_p, 0), stack(outs_p, 1), stack(outs_p, 2), stack(outs_p, 3),
            k_s.reshape(depth, bs, ts, fh, dh), v_s.reshape(depth, bs, ts, fh, dh),
            stack(outs_s, 0), stack(outs_s, 1), stack(outs_s, 2), stack(outs_s, 3))
```

```python
import functools

import jax
import jax.numpy as jnp
from jax import lax
from jax.experimental import pallas as pl
from jax.experimental.pallas import tpu as pltpu

EPS = 1e-6
LANES = 128
VMEM_LIMIT = 56 * 1024 * 1024
BF16 = jnp.bfloat16
F32 = jnp.float32
NEG_INF = float("-inf")
LOG2E = 1.4426950408889634
GATHER_SLOTS = 3
PROJ_ROW_BATCH = 256


def _dot(a, b):
    return jnp.dot(a, b, preferred_element_type=F32)


def _dot_nt(a, b):
    return lax.dot_general(a, b, (((1,), (1,)), ((), ())), preferred_element_type=F32)


def _split3(x):
    hi = x.astype(BF16)
    r = x - hi.astype(F32)
    mid = r.astype(BF16)
    lo = (r - mid.astype(F32)).astype(BF16)
    return hi, mid, lo


def _lane_select(x, idx):
    rows = lax.broadcasted_iota(jnp.int32, (LANES, LANES), 0)
    onehot = jnp.where(rows == idx, 1.0, 0.0).astype(BF16)
    hi, mid, lo = _split3(x)
    return _dot(hi, onehot) + _dot(mid, onehot) + _dot(lo, onehot)


def _log_sigmoid(x):
    return jnp.minimum(x, 0.0) - jnp.log1p(jnp.exp(-jnp.abs(x)))


def _sigmoid(x):
    return 1.0 / (1.0 + jnp.exp(-x))


def _params(*sem, flags=None):
    return pltpu.CompilerParams(dimension_semantics=sem, vmem_limit_bytes=VMEM_LIMIT, flags=flags)


def _pick(n_list, cands):
    for c in cands:
        if all(n % c == 0 for n in n_list):
            return c
    raise ValueError(f"no tile in {cands} divides {n_list}")


def _ada_kernel(c_ref, w_ref, b_ref, o_ref):
    c = c_ref[...]
    sc = (c * _sigmoid(c)).astype(BF16)
    o_ref[...] = _dot(sc, w_ref[...].astype(BF16)) + b_ref[...]


def _ada_call(c_all, w_ada, b_ada):
    depth, d, n6 = w_ada.shape
    r = c_all.shape[0]
    tn = _pick([n6], (1024, 512, 256, 128))
    return pl.pallas_call(
        _ada_kernel,
        grid=(depth, n6 // tn),
        in_specs=[pl.BlockSpec((r, d), lambda l, j: (0, 0)),
                  pl.BlockSpec((None, d, tn), lambda l, j: (l, 0, j)),
                  pl.BlockSpec((None, 1, tn), lambda l, j: (l, 0, j))],
        out_specs=pl.BlockSpec((None, r, tn), lambda l, j: (l, 0, j)),
        out_shape=jax.ShapeDtypeStruct((depth, r, n6), F32),
        compiler_params=_params("arbitrary", "arbitrary"),
        name="ada_mod",
    )(c_all, w_ada, b_ada.reshape(depth, 1, n6))


def _norm_mod(x, g, sc, sh, grp):
    tm, d = x.shape
    ms = jnp.mean(x * x, axis=1, keepdims=True)
    xn = x * lax.rsqrt(ms + EPS) * g
    h3 = xn.reshape(tm // grp, grp, d) * (1.0 + sc) + sh
    return h3.reshape(tm, d)


def _head_norm(z, gain, dh):
    outs = []
    for hh in range(z.shape[1] // dh):
        zz = z[:, hh * dh:(hh + 1) * dh]
        ms = jnp.mean(zz * zz, axis=1, keepdims=True)
        outs.append(zz * lax.rsqrt(ms + EPS) * gain[:, hh * dh:(hh + 1) * dh])
    return outs


_SEG_OUTS = {"hn": 1, "hn_split": 3, "split": 3, "bf16": 1, "sig": 1}


def _proj_kernel(*refs, segs, grp, dh, npt, with_small, n_alias):
    x_ref, g_ref, sc_ref, sh_ref, w_ref, gain_ref = refs[:6]
    pos = 6
    if with_small:
        ws_ref, bs_ref = refs[6:8]
        pos = 8
    pos += n_alias
    h_scr = refs[-1]
    out_refs = refs[pos:-1]
    i = pl.program_id(0)
    j = pl.program_id(1)

    @pl.when(j == 0)
    def _():
        h = _norm_mod(x_ref[...], g_ref[...], sc_ref[...], sh_ref[...], grp).astype(BF16)
        h_scr[...] = h
        if with_small:
            ga_ref, gb_ref = out_refs[-2:]
            zs = _dot(h, ws_ref[...]) + bs_ref[...]
            lane = lax.broadcasted_iota(jnp.int32, (1, LANES), 1)
            ga_ref[...] = jnp.where(lane < with_small, _log_sigmoid(zs[:, :LANES]), 0.0)
            gb_ref[...] = zs[:, LANES:]

    tm = h_scr.shape[0]
    tn = w_ref.shape[1]
    mb = min(tm, PROJ_ROW_BATCH)
    o = 0
    j0 = 0
    for kind, nt in segs:
        outs = out_refs[o:o + _SEG_OUTS[kind]]
        o += _SEG_OUTS[kind]

        def seg_rows(rows, kind, outs):
            z = _dot(h_scr[rows, :], w_ref[...])
            if kind in ("hn", "hn_split"):
                parts = _head_norm(z, gain_ref[...], dh)
            elif kind == "sig":
                parts = [_sigmoid(z)]
            else:
                parts = [z]
            w = tn // len(parts)
            if kind in ("hn_split", "split"):
                p_ref, s_ref, b_ref = outs
                for k, part in enumerate(parts):
                    b_ref[rows, k * w:(k + 1) * w] = part.astype(BF16)

                @pl.when(i < npt)
                def _():
                    for k, part in enumerate(parts):
                        p_ref[rows, k * w:(k + 1) * w] = part

                @pl.when(i >= npt)
                def _():
                    for k, part in enumerate(parts):
                        s_ref[rows, k * w:(k + 1) * w] = part
            else:
                for k, part in enumerate(parts):
                    outs[0][rows, k * w:(k + 1) * w] = part.astype(BF16)

        def seg_body(kind=kind, outs=outs):
            for rb in range(tm // mb):
                seg_rows(slice(rb * mb, (rb + 1) * mb), kind, outs)

        pl.when((j >= j0) & (j < j0 + nt))(seg_body)
        j0 += nt


def _proj_call(x, g, sc, sh, w, gain, segs, *, tm, tn, grp, dh, n_prompt, small=None, name,
               layer=0, depth=1, stacks=None):
    n, d = x.shape
    nt_total = w.shape[1] // tn
    npt = n_prompt // tm
    nst = (n - n_prompt) // tm
    gr = tm // grp

    def clipj(j, j0, nt):
        return jnp.clip(j - j0, 0, nt - 1)

    in_specs = [pl.BlockSpec((tm, d), lambda i, j: (i, 0)),
                pl.BlockSpec((1, d), lambda i, j: (0, 0)),
                pl.BlockSpec((gr, 1, d), lambda i, j: (i, 0, 0)),
                pl.BlockSpec((gr, 1, d), lambda i, j: (i, 0, 0)),
                pl.BlockSpec((d, tn), lambda i, j: (0, j)),
                pl.BlockSpec((1, tn), lambda i, j: (0, j))]
    args = [x, g, sc, sh, w, gain]
    n_gate = 0
    if small is not None:
        ws, bs, n_gate = small
        in_specs += [pl.BlockSpec(ws.shape, lambda i, j: (0, 0)),
                     pl.BlockSpec(bs.shape, lambda i, j: (0, 0))]
        args += [ws, bs]

    out_specs, out_shapes, stack_outs = [], [], []
    j0 = 0
    for kind, nt in segs:
        width = nt * tn
        all_spec = pl.BlockSpec((tm, tn), functools.partial(
            lambda i, j, j0, nt: (i, clipj(j, j0, nt)), j0=j0, nt=nt))
        if kind in ("hn_split", "split"):
            stack_outs.append(len(out_specs))
            out_specs.append(pl.BlockSpec((None, tm, tn), functools.partial(
                lambda i, j, j0, nt: (layer, jnp.minimum(i, npt - 1),
                                      jnp.where(i < npt, clipj(j, j0, nt), nt - 1)), j0=j0, nt=nt)))
            out_shapes.append(jax.ShapeDtypeStruct((depth, n_prompt, width), F32))
            stack_outs.append(len(out_specs))
            out_specs.append(pl.BlockSpec((None, tm, tn), functools.partial(
                lambda i, j, j0, nt: (layer, jnp.maximum(i - npt, 0),
                                      jnp.where(i >= npt, clipj(j, j0, nt), 0)), j0=j0, nt=nt)))
            out_shapes.append(jax.ShapeDtypeStruct((depth, nst * tm, width), F32))
        out_specs.append(all_spec)
        out_shapes.append(jax.ShapeDtypeStruct((n, width), BF16))
        j0 += nt
    assert j0 == nt_total
    if small is not None:
        for _ in range(2):
            out_specs.append(pl.BlockSpec((tm, LANES), lambda i, j: (i, 0)))
            out_shapes.append(jax.ShapeDtypeStruct((n, LANES), F32))
    aliases = {}
    if stacks is not None:
        assert len(stacks) == len(stack_outs)
        for arr, o in zip(stacks, stack_outs):
            aliases[len(args)] = o
            in_specs.append(pl.BlockSpec(memory_space=pl.ANY))
            args.append(arr)

    return pl.pallas_call(
        functools.partial(_proj_kernel, segs=tuple(segs), grp=grp, dh=dh, npt=npt, with_small=n_gate,
                          n_alias=len(aliases)),
        grid=(n // tm, nt_total),
        in_specs=in_specs,
        out_specs=out_specs,
        out_shape=out_shapes,
        scratch_shapes=[pltpu.VMEM((tm, d), BF16)],
        input_output_aliases=aliases,
        compiler_params=_params("arbitrary", "arbitrary"),
        name=name,
    )(*args)


def _scan_kernel(ga_ref, gb_ref, init_ref, lcs_ref, bcum_ref, b_ref, carry):
    c = pl.program_id(1)

    @pl.when(c == 0)
    def _():
        carry[...] = init_ref[...]

    a = ga_ref[...]
    ln = a.shape[0]
    r = lax.broadcasted_iota(jnp.int32, (ln, ln), 0)
    cc = lax.broadcasted_iota(jnp.int32, (ln, ln), 1)
    tri = jnp.where(r >= cc, 1.0, 0.0).astype(BF16)
    hi, mid, lo = _split3(a)
    cs = _dot(tri, hi) + _dot(tri, mid) + _dot(tri, lo)
    lcs_ref[...] = cs + carry[...]
    bcum_ref[...] = cs
    b_ref[...] = gb_ref[...] - cs
    carry[...] = carry[...] + cs[ln - 1:ln, :]


def _scan_call(ga, gb, init, chunk, name):
    b, t, _ = ga.shape
    spec = pl.BlockSpec((None, chunk, LANES), lambda bi, c: (bi, c, 0))
    shp = jax.ShapeDtypeStruct((b, t, LANES), F32)
    return pl.pallas_call(
        _scan_kernel,
        grid=(b, t // chunk),
        in_specs=[spec, spec, pl.BlockSpec((None, 1, LANES), lambda bi, c: (bi, 0, 0))],
        out_specs=[spec, spec, spec],
        out_shape=[shp, shp, shp],
        scratch_shapes=[pltpu.VMEM((1, LANES), F32)],
        compiler_params=_params("arbitrary", "arbitrary"),
        name=name,
    )(ga, gb, init)


def _bias_lanes(x_row, own):
    x = jnp.broadcast_to(x_row, (LANES, x_row.shape[1])).T
    hi, mid, lo = _split3(x)
    lane = lax.broadcasted_iota(jnp.int32, x.shape, 1)
    base = 0 if own else 3
    const = jnp.where((lane >= 3 - base) & (lane < 6 - base), -1.0 if own else 1.0, 0.0).astype(BF16)
    return jnp.where(lane == base, hi, jnp.where(lane == base + 1, mid, jnp.where(lane == base + 2, lo, const)))


def _fox_kernel(q_ref, k_ref, v_ref, lrow_ref, o_ref, vt_scr, ka_scr, qa_scr, m_scr, l_scr, acc_scr,
                sa_scr, ca_scr, sb_scr, cb_scr, *, tb, nh, dh):
    qi = pl.program_id(2)
    nblk = k_ref.shape[0] // tb

    @pl.when(qi == 0)
    def _():
        def prep(c, carry):
            off = pl.multiple_of(c * tb, tb)
            for hh in range(nh):
                v = v_ref[pl.ds(off, tb), hh * dh:(hh + 1) * dh]
                vt_scr[hh, c] = v.astype(F32).T.astype(BF16)
                ka_scr[hh, c, :, :dh] = k_ref[pl.ds(off, tb), hh * dh:(hh + 1) * dh]
                ka_scr[hh, c, :, dh:] = _bias_lanes(lrow_ref[hh, c], False)
            return carry

        lax.fori_loop(0, nblk, prep, 0)

    for hh in range(nh):
        m_scr[hh] = jnp.full((1, tb), NEG_INF, F32)
        l_scr[hh] = jnp.zeros((1, tb), F32)
        acc_scr[hh] = jnp.zeros((dh, tb), F32)
        qa_scr[hh, :, :dh] = q_ref[:, hh * dh:(hh + 1) * dh]
        qa_scr[hh, :, dh:] = _bias_lanes(lrow_ref[hh, qi], True)

    def scores(ki, s_buf, c_buf, masked):
        for hh in range(nh):
            st = _dot_nt(ka_scr[hh, ki], qa_scr[hh])
            if masked:
                r = lax.broadcasted_iota(jnp.int32, (tb, tb), 0)
                c = lax.broadcasted_iota(jnp.int32, (tb, tb), 1)
                st = jnp.where(r <= c, st, NEG_INF)
            s_buf[hh] = st
            c_buf[hh] = jnp.max(st, axis=0, keepdims=True)

    def accumulate(ki, s_buf, c_buf):
        for hh in range(nh):
            m_prev = m_scr[hh]
            m_new = jnp.maximum(m_prev, c_buf[hh])
            p = jnp.exp2(s_buf[hh] - m_new)
            alpha = jnp.exp2(m_prev - m_new)
            l_scr[hh] = alpha * l_scr[hh] + jnp.sum(p, axis=0, keepdims=True)
            acc_scr[hh] = alpha * acc_scr[hh] + _dot(vt_scr[hh, ki], p.astype(BF16))
            m_scr[hh] = m_new

    scores(qi, sa_scr, ca_scr, True)

    def pair(tstep):
        k1 = qi - 1 - 2 * tstep
        scores(k1, sb_scr, cb_scr, False)
        accumulate(k1 + 1, sa_scr, ca_scr)
        scores(k1 - 1, sa_scr, ca_scr, False)
        accumulate(k1, sb_scr, cb_scr)

    def quad(u, carry):
        pair(2 * u)
        pair(2 * u + 1)
        return carry

    n_pairs = qi // 2
    lax.fori_loop(0, n_pairs // 2, quad, 0)

    @pl.when(n_pairs % 2 == 1)
    def _():
        pair(n_pairs - 1)

    @pl.when(qi % 2 == 1)
    def _():
        scores(0, sb_scr, cb_scr, False)
        accumulate(1, sa_scr, ca_scr)
        accumulate(0, sb_scr, cb_scr)

    @pl.when(qi % 2 == 0)
    def _():
        accumulate(0, sa_scr, ca_scr)

    for hh in range(nh):
        o = acc_scr[hh] * (1.0 / l_scr[hh])
        o_ref[:, hh * dh:(hh + 1) * dh] = o.T.astype(o_ref.dtype)


def _fox_prompt_call(q, k, v, lrows, *, b, t, fh, dh, tb, nh):
    n = q.shape[0]
    nq = t // tb
    hw = nh * dh
    return pl.pallas_call(
        functools.partial(_fox_kernel, tb=tb, nh=nh, dh=dh),
        grid=(b, fh // nh, nq),
        in_specs=[pl.BlockSpec((tb, hw), lambda bi, h, qi: (bi * nq + qi, h)),
                  pl.BlockSpec((t, hw), lambda bi, h, qi: (bi, h)),
                  pl.BlockSpec((t, hw), lambda bi, h, qi: (bi, h)),
                  pl.BlockSpec((nh, nq, 1, tb), lambda bi, h, qi: (bi * (fh // nh) + h, 0, 0, 0))],
        out_specs=pl.BlockSpec((tb, hw), lambda bi, h, qi: (bi * nq + qi, h)),
        out_shape=jax.ShapeDtypeStruct((n, fh * dh), BF16),
        scratch_shapes=[pltpu.VMEM((nh, nq, dh, tb), BF16), pltpu.VMEM((nh, nq, tb, dh + LANES), BF16),
                        pltpu.VMEM((nh, tb, dh + LANES), BF16),
                        pltpu.VMEM((nh, 1, tb), F32), pltpu.VMEM((nh, 1, tb), F32),
                        pltpu.VMEM((nh, dh, tb), F32),
                        pltpu.VMEM((nh, tb, tb), F32), pltpu.VMEM((nh, 1, tb), F32),
                        pltpu.VMEM((nh, tb, tb), F32), pltpu.VMEM((nh, 1, tb), F32)],
        compiler_params=_params("arbitrary", "arbitrary", "arbitrary"),
        name="fox_prompt",
    )(q, k, v, lrows)


def _fox_sample_kernel(q_ref, kp_ref, vp_ref, kn_ref, vn_ref, lcs_ref, lkp_ref, lkn_ref, y_any, o_ref, *, fh, dh):
    del y_any
    ts = q_ref.shape[0]
    lcs = lcs_ref[...]
    r = lax.broadcasted_iota(jnp.int32, (ts, ts), 0)
    c = lax.broadcasted_iota(jnp.int32, (ts, ts), 1)
    for h in range(fh):
        cols = slice(h * dh, (h + 1) * dh)
        q = q_ref[:, cols]
        lq = _lane_select(lcs, h)[:, :1] * LOG2E
        sp = _dot_nt(q, kp_ref[:, h, :].astype(BF16)) + lq - lkp_ref[h:h + 1, :]
        sn = _dot_nt(q, kn_ref[:, cols]) + lq - lkn_ref[h:h + 1, :]
        sn = jnp.where(r >= c, sn, NEG_INF)
        m = jnp.maximum(jnp.max(sp, axis=1, keepdims=True), jnp.max(sn, axis=1, keepdims=True))
        pp = jnp.exp2(sp - m)
        pn = jnp.exp2(sn - m)
        den = jnp.sum(pp, axis=1, keepdims=True) + jnp.sum(pn, axis=1, keepdims=True)
        acc = _dot(pp.astype(BF16), vp_ref[:, h, :].astype(BF16)) + _dot(pn.astype(BF16), vn_ref[:, cols])
        o_ref[:, cols] = (acc / den).astype(o_ref.dtype)


def _fox_sample_call(y, q, kb, vb, cache_k, cache_v, lcs_s, lkp_rows, lkn_rows, *, layer, n_prompt, bs, ts, fh, dh):
    p = cache_k.shape[2]
    r0 = n_prompt // ts
    fw = fh * dh
    return pl.pallas_call(
        functools.partial(_fox_sample_kernel, fh=fh, dh=dh),
        grid=(bs,),
        in_specs=[pl.BlockSpec((ts, fw), lambda bi: (r0 + bi, 0)),
                  pl.BlockSpec((None, None, p, fh, dh), lambda bi: (layer, bi, 0, 0, 0)),
                  pl.BlockSpec((None, None, p, fh, dh), lambda bi: (layer, bi, 0, 0, 0)),
                  pl.BlockSpec((ts, fw), lambda bi: (r0 + bi, 0)),
                  pl.BlockSpec((ts, fw), lambda bi: (r0 + bi, 0)),
                  pl.BlockSpec((ts, LANES), lambda bi: (bi, 0)),
                  pl.BlockSpec((None, fh, p), lambda bi: (bi, 0, 0)),
                  pl.BlockSpec((None, fh, ts), lambda bi: (bi, 0, 0)),
                  pl.BlockSpec(memory_space=pl.ANY)],
        out_specs=pl.BlockSpec((ts, fw), lambda bi: (r0 + bi, 0)),
        out_shape=jax.ShapeDtypeStruct(y.shape, y.dtype),
        input_output_aliases={8: 0},
        compiler_params=_params("arbitrary"),
        name="fox_sample",
    )(q, cache_k, cache_v, kb, vb, lcs_s, lkp_rows, lkn_rows, y)


def _mlstm_kernel(*refs, hoff, aliased):
    if aliased:
        refs = refs[:11] + refs[12:]
    (q_ref, k_ref, v_ref, o_ref, bcum_ref, bnat_ref, brow_ref, g_ref, c0_ref, n0_ref, m0_ref,
     y_ref, c1_ref, n1_ref, m1_ref, c_scr, n_scr, m_scr) = refs
    h = pl.program_id(1)
    c = pl.program_id(2)
    nc = pl.num_programs(2)

    @pl.when(c == 0)
    def _():
        c_scr[...] = c0_ref[...]
        n_scr[...] = n0_ref[...]
        m_scr[...] = m0_ref[...]

    q = q_ref[...]
    k = k_ref[...]
    v = v_ref[...]
    ln = q.shape[0]
    bcum = _lane_select(bcum_ref[...], hoff + h)[:, :1]
    bcol = _lane_select(bnat_ref[...], hoff + h)[:, :1]
    brow = brow_ref[...]
    m0 = m_scr[:, :1]
    r = lax.broadcasted_iota(jnp.int32, (ln, ln), 0)
    cc = lax.broadcasted_iota(jnp.int32, (ln, ln), 1)
    bm = jnp.where(r >= cc, brow, NEG_INF)
    g = jnp.maximum(m0, jnp.max(bm, axis=1, keepdims=True))
    dmat = jnp.exp(bm - g)
    s = _dot_nt(q, k) * dmat
    w_inter = jnp.exp(m0 - g)
    cmat = c_scr[...]
    num = _dot(s.astype(BF16), v) + _dot(q, cmat.astype(BF16)) * w_inter
    qn = jnp.sum(q.astype(F32) * n_scr[...], axis=1, keepdims=True)
    den = jnp.sum(s, axis=1, keepdims=True) + w_inter * qn
    den = jnp.maximum(jnp.abs(den), jnp.exp(-(bcum + g)))
    hval = num * (1.0 / den)
    ms = jnp.mean(hval * hval, axis=1, keepdims=True)
    y = hval * lax.rsqrt(ms + EPS) * g_ref[...] * o_ref[...].astype(F32)
    y_ref[...] = y.astype(y_ref.dtype)

    g_last = g[ln - 1:ln, :]
    w_key = jnp.exp(bcol - g_last)
    f_tot = jnp.exp(m0 - g_last)
    kw = k.astype(F32) * w_key
    c_new = f_tot * cmat + _dot(kw.T.astype(BF16), v)
    n_new = f_tot * n_scr[...] + jnp.sum(kw, axis=0, keepdims=True)
    m_new = jnp.broadcast_to(bcum[ln - 1:ln, :] + g_last, m_scr.shape)
    c_scr[...] = c_new
    n_scr[...] = n_new
    m_scr[...] = m_new

    @pl.when(c == nc - 1)
    def _():
        c1_ref[...] = c_new
        n1_ref[...] = n_new
        m1_ref[...] = m_new


def _mlstm_call(mq, mk, mv, mo, bcum, bnat, brows, gain, c0, n0, m0, *, row0, nb, nc, chunk, mh, dqk, dv,
                hoff, y_prev, name):
    n = mq.shape[0]
    rb0 = row0 // chunk

    def rows(bi, h, c):
        return rb0 + bi * nc + c

    in_specs = [pl.BlockSpec((chunk, dqk), lambda bi, h, c: (rows(bi, h, c), h)),
                pl.BlockSpec((chunk, dqk), lambda bi, h, c: (rows(bi, h, c), h)),
                pl.BlockSpec((chunk, dv), lambda bi, h, c: (rows(bi, h, c), h)),
                pl.BlockSpec((chunk, dv), lambda bi, h, c: (rows(bi, h, c), h)),
                pl.BlockSpec((chunk, LANES), lambda bi, h, c: (bi * nc + c, 0)),
                pl.BlockSpec((chunk, LANES), lambda bi, h, c: (bi * nc + c, 0)),
                pl.BlockSpec((None, None, 1, chunk), lambda bi, h, c: (bi * mh + h, c, 0, 0)),
                pl.BlockSpec((None, 1, dv), lambda bi, h, c: (h, 0, 0)),
                pl.BlockSpec((None, None, dqk, dv), lambda bi, h, c: (bi, h, 0, 0)),
                pl.BlockSpec((None, None, 1, dqk), lambda bi, h, c: (bi, h, 0, 0)),
                pl.BlockSpec((None, None, 1, LANES), lambda bi, h, c: (bi, h, 0, 0))]
    args = [mq, mk, mv, mo, bcum, bnat, brows, gain, c0, n0, m0]
    aliases = {}
    if y_prev is not None:
        in_specs.append(pl.BlockSpec(memory_space=pl.ANY))
        args.append(y_prev)
        aliases = {len(args) - 1: 0}
    return pl.pallas_call(
        functools.partial(_mlstm_kernel, hoff=hoff, aliased=y_prev is not None),
        grid=(nb, mh, nc),
        in_specs=in_specs,
        out_specs=[pl.BlockSpec((chunk, dv), lambda bi, h, c: (rows(bi, h, c), h)),
                   pl.BlockSpec((None, None, dqk, dv), lambda bi, h, c: (bi, h, 0, 0)),
                   pl.BlockSpec((None, None, 1, dqk), lambda bi, h, c: (bi, h, 0, 0)),
                   pl.BlockSpec((None, None, 1, LANES), lambda bi, h, c: (bi, h, 0, 0))],
        out_shape=[jax.ShapeDtypeStruct((n, mh * dv), BF16),
                   jax.ShapeDtypeStruct((nb, mh, dqk, dv), F32),
                   jax.ShapeDtypeStruct((nb, mh, 1, dqk), F32),
                   jax.ShapeDtypeStruct((nb, mh, 1, LANES), F32)],
        scratch_shapes=[pltpu.VMEM((dqk, dv), F32), pltpu.VMEM((1, dqk), F32), pltpu.VMEM((1, LANES), F32)],
        input_output_aliases=aliases,
        compiler_params=_params("arbitrary", "arbitrary", "arbitrary"),
        name=name,
    )(*args)


def _route(logits, b_router, n_exp, epg):
    n_grp = n_exp // epg
    lane = lax.broadcasted_iota(jnp.int32, logits.shape, 1)
    valid = lane < n_exp
    pos = lane % epg
    grp = lane // epg
    s = _sigmoid(logits)
    sb = jnp.where(valid, s + b_router, NEG_INF)

    def nbr(x, d):
        return pltpu.roll(x, (-d) % LANES, axis=1)

    def count_beaten(x, idx, span, step):
        cnt = jnp.zeros(x.shape, F32)
        for d in range(-(span - 1), span):
            if d == 0:
                continue
            other = nbr(x, d * step)
            inside = (idx + d >= 0) & (idx + d < span)
            ahead = (other > x) | ((other == x) & (d < 0))
            cnt = cnt + jnp.where(inside & ahead, 1.0, 0.0)
        return cnt

    top2 = count_beaten(sb, pos, epg, 1) < 2.0
    t2v = jnp.where(top2, sb, 0.0)
    t2v = jnp.where(valid, t2v, NEG_INF)
    gscore = t2v
    for d in range(-(epg - 1), epg):
        if d == 0:
            continue
        inside = (pos + d >= 0) & (pos + d < epg)
        gscore = gscore + jnp.where(inside, nbr(t2v, d), 0.0)
    best_grp = count_beaten(gscore, grp, n_grp, epg) < 1.0
    sel = valid & best_grp & top2
    w = jnp.where(sel, s, 0.0)
    return w / jnp.sum(w, axis=1, keepdims=True), jnp.where(sel, 1.0, 0.0)


def _merge_kernel(x_ref, g1n_ref, sc1_ref, sh1_ref, gt1_ref, yf_ref, ym_ref, wga_ref, wgb_ref, bga_ref, bgb_ref,
                  wo_ref, g2n_ref, sc2_ref, sh2_ref, wrh_ref, wrl_ref, br_ref,
                  x1_ref, h2_ref, gates_ref, sel_ref, h_scr, acc_scr, *, grp, n_exp, epg):
    j = pl.program_id(1)
    nj = pl.num_programs(1)

    @pl.when(j == 0)
    def _():
        h_scr[...] = _norm_mod(x_ref[...], g1n_ref[...], sc1_ref[...], sh1_ref[...], grp).astype(BF16)
        acc_scr[...] = jnp.zeros(acc_scr.shape, F32)

    mb = min(h_scr.shape[0], PROJ_ROW_BATCH)
    for rb in range(h_scr.shape[0] // mb):
        rows = slice(rb * mb, (rb + 1) * mb)
        h = h_scr[rows, :]
        ga = _dot(h, wga_ref[...]) + bga_ref[...]
        gb = _dot(h, wgb_ref[...]) + bgb_ref[...]
        mix = _sigmoid(ga) * yf_ref[rows, :].astype(F32) + _sigmoid(gb) * ym_ref[rows, :].astype(F32)
        acc_scr[rows, :] += _dot(mix.astype(BF16), wo_ref[...])

    @pl.when(j == nj - 1)
    def _():
        tm, d = x_ref.shape
        acc3 = acc_scr[...].reshape(tm // grp, grp, d) * gt1_ref[...]
        x1 = x_ref[...] + acc3.reshape(tm, d)
        x1_ref[...] = x1
        h2 = _norm_mod(x1, g2n_ref[...], sc2_ref[...], sh2_ref[...], grp)
        hi = h2.astype(BF16)
        lo = (h2 - hi.astype(F32)).astype(BF16)
        h2_ref[...] = h2
        logits = _dot(hi, wrh_ref[...]) + _dot(lo, wrh_ref[...]) + _dot(hi, wrl_ref[...])
        gates_ref[...], sel_ref[...] = _route(logits, br_ref[...], n_exp, epg)


def _merge_call(x, g1n, sc1, sh1, gt1, yf, ym, wgate, bgate, wout, g2n, sc2, sh2, wr_hi, wr_lo, br,
                *, tm, tc, grp, n_exp, epg):
    n, d = x.shape
    nj = d // tc
    gr = tm // grp
    row = pl.BlockSpec((tm, d), lambda i, j: (i, 0))
    vec = pl.BlockSpec((1, d), lambda i, j: (0, 0))
    mod = pl.BlockSpec((gr, 1, d), lambda i, j: (i, 0, 0))
    return pl.pallas_call(
        functools.partial(_merge_kernel, grp=grp, n_exp=n_exp, epg=epg),
        grid=(n // tm, nj),
        in_specs=[row, vec, mod, mod, mod,
                  pl.BlockSpec((tm, tc), lambda i, j: (i, j)),
                  pl.BlockSpec((tm, tc), lambda i, j: (i, j)),
                  pl.BlockSpec((d, tc), lambda i, j: (0, j)),
                  pl.BlockSpec((d, tc), lambda i, j: (0, nj + j)),
                  pl.BlockSpec((1, tc), lambda i, j: (0, j)),
                  pl.BlockSpec((1, tc), lambda i, j: (0, nj + j)),
                  pl.BlockSpec((tc, d), lambda i, j: (j, 0)),
                  vec, mod, mod,
                  pl.BlockSpec((d, LANES), lambda i, j: (0, 0)),
                  pl.BlockSpec((d, LANES), lambda i, j: (0, 0)),
                  pl.BlockSpec((1, LANES), lambda i, j: (0, 0))],
        out_specs=[row, row, pl.BlockSpec((tm, LANES), lambda i, j: (i, 0)),
                   pl.BlockSpec((tm, LANES), lambda i, j: (i, 0))],
        out_shape=[jax.ShapeDtypeStruct((n, d), F32), jax.ShapeDtypeStruct((n, d), F32),
                   jax.ShapeDtypeStruct((n, LANES), F32), jax.ShapeDtypeStruct((n, LANES), F32)],
        scratch_shapes=[pltpu.VMEM((tm, d), BF16), pltpu.VMEM((tm, d), F32)],
        compiler_params=_params("arbitrary", "arbitrary"),
        name="merge_out",
    )(x, g1n, sc1, sh1, gt1, yf, ym, wgate, wgate, bgate, bgate, wout, g2n, sc2, sh2, wr_hi, wr_lo, br)


def _slot_table_kernel(p0_ref, p1_ref, lo_ref, hi_ref, idx_ref, *, n_tok):
    for k in range(lo_ref.shape[0]):
        def clear(r, c):
            idx_ref[r] = 0
            return c

        lax.fori_loop(lo_ref[k], hi_ref[k], clear, 0)

    def fill(r, c):
        idx_ref[p0_ref[r]] = r
        idx_ref[p1_ref[r]] = r
        return c

    lax.fori_loop(0, n_tok, fill, 0, unroll=8)


def _slot_table_call(pos0, pos1, pad_lo, pad_hi, n_slot):
    n_tok = pos0.shape[0]
    smem = pl.BlockSpec(memory_space=pltpu.SMEM)
    return pl.pallas_call(
        functools.partial(_slot_table_kernel, n_tok=n_tok),
        in_specs=[smem, smem, smem, smem],
        out_specs=smem,
        out_shape=jax.ShapeDtypeStruct((n_slot,), jnp.int32),
        name="moe_slots",
    )(pos0, pos1, pad_lo, pad_hi)


def _gather_rows(src_any, idx_ref, base, dst, sem, n_rows, inline=False):
    def issue(r, c):
        pltpu.make_async_copy(src_any.at[pl.ds(idx_ref[base + r], 1)], dst.at[pl.ds(r, 1)], sem).start()
        return c

    if inline:
        for r in range(n_rows):
            issue(r, 0)
    else:
        lax.fori_loop(0, n_rows, issue, 0, unroll=8)


def _wait_rows(src_any, dst, sem):
    pltpu.make_async_copy(src_any.at[pl.ds(0, dst.shape[0])], dst, sem).wait()


def _expert_kernel(idx_ref, te_ref, nu_ref, h_any, wg_ref, wu_ref, wd_ref, ys_ref,
                   xbuf, wgb, wub, wdb, sem, *, tmx):
    t = pl.program_id(0)
    nt = pl.num_programs(0)
    ahead = GATHER_SLOTS - 1
    slot = t % GATHER_SLOTS

    @pl.when(t == 0)
    def _():
        for k in range(ahead):
            _gather_rows(h_any, idx_ref, jnp.minimum(k, nt - 1) * tmx, xbuf.at[k], sem.at[k], tmx)

    @pl.when((t == 0) | (te_ref[t] != te_ref[jnp.maximum(t - 1, 0)]))
    def _():
        wgb[...] = wg_ref[...].astype(BF16)
        wub[...] = wu_ref[...].astype(BF16)
        wdb[...] = wd_ref[...].astype(BF16)

    _wait_rows(h_any, xbuf.at[slot], sem.at[slot])
    nxt = jnp.minimum(t + ahead, nt - 1) * tmx
    nslot = (t + ahead) % GATHER_SLOTS

    @pl.when(t < nu_ref[0])
    def _():
        x = xbuf[slot].astype(BF16)
        a = _dot(x, wgb[...])
        u = _dot(x, wub[...])
        y = (a * _sigmoid(a)) * u
        ys_ref[...] = _dot(y.astype(BF16), wdb[...])
        _gather_rows(h_any, idx_ref, nxt, xbuf.at[nslot], sem.at[nslot], tmx, inline=True)

    @pl.when(t >= nu_ref[0])
    def _():
        ys_ref[...] = jnp.zeros(ys_ref.shape, F32)
        _gather_rows(h_any, idx_ref, nxt, xbuf.at[nslot], sem.at[nslot], tmx)

    @pl.when(t == nt - 1)
    def _():
        for k in range(1, GATHER_SLOTS):
            s = (t + k) % GATHER_SLOTS
            _wait_rows(h_any, xbuf.at[s], sem.at[s])


def _expert_call(idx, tile_expert, n_used, h2, wg, wu, wd, *, tmx, layer):
    n, d = h2.shape
    de = wg.shape[-1]
    nt = tile_expert.shape[0]
    grid_spec = pltpu.PrefetchScalarGridSpec(
        num_scalar_prefetch=3,
        grid=(nt,),
        in_specs=[pl.BlockSpec(memory_space=pl.ANY),
                  pl.BlockSpec((None, None, d, de), lambda t, idx, te, nu: (layer, te[t], 0, 0)),
                  pl.BlockSpec((None, None, d, de), lambda t, idx, te, nu: (layer, te[t], 0, 0)),
                  pl.BlockSpec((None, None, de, d), lambda t, idx, te, nu: (layer, te[t], 0, 0))],
        out_specs=pl.BlockSpec((tmx, d), lambda t, idx, te, nu: (t, 0)),
        scratch_shapes=[pltpu.VMEM((GATHER_SLOTS, tmx, d), F32), pltpu.VMEM((d, de), BF16),
                        pltpu.VMEM((d, de), BF16), pltpu.VMEM((de, d), BF16),
                        pltpu.SemaphoreType.DMA((GATHER_SLOTS,))])
    return pl.pallas_call(
        functools.partial(_expert_kernel, tmx=tmx),
        grid_spec=grid_spec,
        out_shape=jax.ShapeDtypeStruct((nt * tmx, d), F32),
        compiler_params=_params("arbitrary"),
        name="moe_experts",
    )(idx, tile_expert, n_used, h2, wg, wu, wd)


def _combine_kernel(p0_ref, p1_ref, ys_any, x1_ref, gt2_ref, wp_ref, *refs, tmc, grp, npt):
    out_refs, (buf, sem) = refs[:-2], refs[-2:]
    i = pl.program_id(0)
    nt = pl.num_programs(0)
    slot = i % 2

    def gather(base, s, inline):
        _gather_rows(ys_any, p0_ref, base, buf.at[s, 0], sem.at[s], tmc, inline)
        _gather_rows(ys_any, p1_ref, base, buf.at[s, 1], sem.at[s], tmc, inline)

    def wait(s):
        _wait_rows(ys_any, buf.at[s, 0], sem.at[s])
        _wait_rows(ys_any, buf.at[s, 1], sem.at[s])

    @pl.when(i == 0)
    def _():
        gather(0, 0, False)

    wait(slot)
    gather(jnp.minimum(i + 1, nt - 1) * tmc, 1 - slot, True)
    w = wp_ref[...]
    moe = w[:, 0:1] * buf[slot, 0] + w[:, 1:2] * buf[slot, 1]
    tm, d = moe.shape
    out = x1_ref[...] + (moe.reshape(tm // grp, grp, d) * gt2_ref[...]).reshape(tm, d)
    if len(out_refs) == 1:
        out_refs[0][...] = out
    else:
        @pl.when(i < npt)
        def _():
            out_refs[0][...] = out

        @pl.when(i >= npt)
        def _():
            out_refs[1][...] = out

    @pl.when(i == nt - 1)
    def _():
        wait(1 - slot)


def _combine_call(pos0, pos1, ys, x1, gt2, wpair, *, tmc, grp, n_prompt, split):
    n, d = x1.shape
    gr = tmc // grp
    npt = n_prompt // tmc
    row = lambda i, p0, p1: (i, 0)
    if split:
        out_specs = [pl.BlockSpec((tmc, d), lambda i, p0, p1: (jnp.minimum(i, npt - 1), 0)),
                     pl.BlockSpec((tmc, d), lambda i, p0, p1: (jnp.maximum(i - npt, 0), 0))]
        out_shape = [jax.ShapeDtypeStruct((n_prompt, d), F32), jax.ShapeDtypeStruct((n - n_prompt, d), F32)]
    else:
        out_specs = [pl.BlockSpec((tmc, d), row)]
        out_shape = [jax.ShapeDtypeStruct((n, d), F32)]
    grid_spec = pltpu.PrefetchScalarGridSpec(
        num_scalar_prefetch=2,
        grid=(n // tmc,),
        in_specs=[pl.BlockSpec(memory_space=pl.ANY),
                  pl.BlockSpec((tmc, d), row),
                  pl.BlockSpec((gr, 1, d), lambda i, p0, p1: (i, 0, 0)),
                  pl.BlockSpec((tmc, LANES), row)],
        out_specs=out_specs,
        scratch_shapes=[pltpu.VMEM((2, 2, tmc, d), F32), pltpu.SemaphoreType.DMA((2,))])
    return pl.pallas_call(
        functools.partial(_combine_kernel, tmc=tmc, grp=grp, npt=npt),
        grid_spec=grid_spec,
        out_shape=out_shape,
        compiler_params=_params("arbitrary"),
        name="moe_combine",
    )(pos0, pos1, ys, x1, gt2, wpair)


def _moe_sparse(h2, gates, sel, x1, gt2, wg, wu, wd, *, layer, tmx, tmc, grp, n_prompt, split):
    n, d = x1.shape
    n_exp = wg.shape[1]
    nt = -(-(2 * n + n_exp * (tmx - 1)) // tmx)
    n_slot = nt * tmx
    cs = _scan_call(sel.reshape(1, n, LANES), sel.reshape(1, n, LANES), jnp.zeros((1, 1, LANES), F32),
                    _pick([n], (256, 128, 64)), "scan_route")[0].reshape(n, LANES)
    counts = cs[n - 1, :n_exp].astype(jnp.int32)
    padded = (counts + tmx - 1) // tmx * tmx
    ends = jnp.cumsum(padded)
    off = jnp.zeros((LANES,), jnp.int32).at[:n_exp].set(ends - padded)
    dest = off[None, :] + cs.astype(jnp.int32) - 1
    chosen = sel > 0.0
    pos0 = jnp.min(jnp.where(chosen, dest, n_slot), axis=1)
    pos1 = jnp.max(jnp.where(chosen, dest, -1), axis=1)
    w0 = jnp.sum(jnp.where(chosen & (dest == pos0[:, None]), gates, 0.0), axis=1)
    w1 = jnp.sum(jnp.where(chosen & (dest == pos1[:, None]), gates, 0.0), axis=1)
    wpair = jnp.zeros((n, LANES), F32).at[:, 0].set(w0).at[:, 1].set(w1)
    tile_expert = jnp.minimum(jnp.sum(jnp.arange(nt)[:, None] * tmx >= ends[None, :], axis=1), n_exp - 1)
    n_used = (ends[n_exp - 1] // tmx).reshape(1).astype(jnp.int32)
    pad_lo = jnp.concatenate([ends - padded + counts, ends[n_exp - 1:]]).astype(jnp.int32)
    pad_hi = jnp.concatenate([ends, jnp.full((1,), n_slot, jnp.int32)]).astype(jnp.int32)
    idx = _slot_table_call(pos0, pos1, pad_lo, pad_hi, n_slot)
    ys = _expert_call(idx, tile_expert.astype(jnp.int32), n_used, h2, wg, wu, wd, tmx=tmx, layer=layer)
    return _combine_call(pos0, pos1, ys, x1, gt2, wpair, tmc=tmc, grp=grp, n_prompt=n_prompt, split=split)


def kernel(x_prompt, x_sample, cache_fox_k, cache_fox_v, cache_fox_logf, state_mlstm_c, state_mlstm_n, state_mlstm_m, c_prompt, c_sample, norm1_g, norm2_g, w_ada, b_ada, w_in, fox_qn_g, fox_kn_g, fox_fb, ml_ib, ml_fb, ml_out_g, w_gate, b_gate, w_out, w_router, b_router, w_e_gate, w_e_up, w_e_down):
    b, t, d = x_prompt.shape
    bs, ts, _ = x_sample.shape
    depth = w_in.shape[0]
    past = cache_fox_k.shape[2]
    fh = fox_fb.shape[-1]
    dh = d // fh
    fw = fh * dh
    mh = ml_ib.shape[-1]
    dqk, dv = state_mlstm_c.shape[-2:]
    qw, vw = mh * dqk, mh * dv
    n_exp = w_router.shape[-1]
    epg = n_exp // 4
    n_p, n_s = b * t, bs * ts
    n = n_p + n_s
    grp = ts
    tm_a = _pick([n_p, n_s], (1024, 512, 256, 128))
    tm_b = _pick([n_p, n_s], (512, 256, 128))
    tn = _pick([fw, qw, vw], (512, 256, 128))
    tc = _pick([d], (512, 256, 128))
    tb = _pick([t], (512, 256, 128))
    nh = 2 if fh % 2 == 0 else 1
    tmx = 256
    tmc = _pick([n_p, n_s], (256, 128))
    lp = _pick([t], (256, 128, 64))
    pc = _pick([past], (512, 256, 128, 64))
    assert fh + mh <= LANES and n_exp <= LANES and t % grp == 0 and tm_b % grp == 0

    x = jnp.concatenate([x_prompt.reshape(n_p, d), x_sample.reshape(n_s, d)], axis=0)
    n_seq = b + bs
    r_pad = -(-n_seq // 8) * 8
    c_all = jnp.concatenate([c_prompt, c_sample, jnp.zeros((r_pad - n_seq, d), F32)], axis=0)
    mod = _ada_call(c_all, w_ada, b_ada)
    seq_of_group = jnp.concatenate([jnp.repeat(jnp.arange(b), t // grp), b + jnp.arange(bs)])

    cuts = [0]
    for wdt in (fw, fw, fw, fh, qw, qw, vw, mh, mh, vw):
        cuts.append(cuts[-1] + wdt)
    (c_fq, c_fk, c_fv, c_ff, c_mq, c_mk, c_mv, c_mi, c_mf, c_mo, _) = cuts

    wr_hi = jnp.zeros((d, LANES), F32).at[:, :n_exp].set(w_router)
    wr_lo = (wr_hi - wr_hi.astype(BF16).astype(F32)).astype(BF16)
    wr_hi = wr_hi.astype(BF16)
    br = jnp.zeros((1, LANES), F32).at[0, :n_exp].set(b_router)
    zeros_c = jnp.zeros((b, mh, dqk, dv), F32)
    zeros_n = jnp.zeros((b, mh, 1, dqk), F32)
    zeros_m = jnp.zeros((b, mh, 1, LANES), F32)

    outs_p, outs_s = [], []
    kv_stacks = None
    for l in range(depth):
        modl = mod[l][seq_of_group]
        sh1, sc1, gt1, sh2, sc2, gt2 = [modl[:, k * d:(k + 1) * d].reshape(n // grp, 1, d) for k in range(6)]
        wl = w_in[l]
        w1 = wl[:, :c_ff].astype(BF16)
        w2 = jnp.concatenate([wl[:, c_mq:c_mk] * (dqk ** -0.5), wl[:, c_mk:c_mi], wl[:, c_mo:]], axis=1).astype(BF16)
        ws = jnp.zeros((d, 2 * LANES), F32)
        ws = ws.at[:, :fh].set(wl[:, c_ff:c_mq]).at[:, fh:fh + mh].set(wl[:, c_mf:c_mo])
        ws = ws.at[:, LANES + fh:LANES + fh + mh].set(wl[:, c_mi:c_mf]).astype(BF16)
        bsm = jnp.zeros((1, 2 * LANES), F32)
        bsm = bsm.at[0, :fh].set(fox_fb[l]).at[0, fh:fh + mh].set(ml_fb[l])
        bsm = bsm.at[0, LANES + fh:LANES + fh + mh].set(ml_ib[l])
        gain1 = jnp.concatenate([jnp.tile(fox_qn_g[l] * (LOG2E * dh ** -0.5), fh), jnp.tile(fox_kn_g[l], fh),
                                 jnp.ones((fw,), F32)]).reshape(1, 3 * fw)
        g1n = norm1_g[l].reshape(1, d)
        g2n = norm2_g[l].reshape(1, d)

        q_b, k_p, k_s, k_b, v_p, v_s, v_b, g_a, g_b = _proj_call(
            x, g1n, sc1, sh1, w1, gain1, [("hn", fw // tn), ("hn_split", fw // tn), ("split", fw // tn)],
            tm=tm_a, tn=tn, grp=grp, dh=dh, n_prompt=n_p, small=(ws, bsm, fh + mh), name="proj_fox",
            layer=l, depth=depth, stacks=kv_stacks)
        kv_stacks = [k_p, k_s, v_p, v_s]
        m_q, m_k, m_v, m_o = _proj_call(
            x, g1n, sc1, sh1, w2, jnp.ones((1, w2.shape[1]), F32),
            [("bf16", qw // tn), ("bf16", qw // tn), ("bf16", vw // tn), ("sig", vw // tn)],
            tm=tm_a, tn=tn, grp=grp, dh=dh, n_prompt=n_p, name="proj_mlstm")

        lcs_p, bcum_p, bnat_p = _scan_call(g_a[:n_p].reshape(b, t, LANES), g_b[:n_p].reshape(b, t, LANES),
                                           jnp.zeros((b, 1, LANES), F32), lp, "scan_prompt")
        past_lf = jnp.zeros((bs, past, LANES), F32).at[:, :, :fh].set(cache_fox_logf[l])
        lcs_c, _, _ = _scan_call(past_lf, past_lf, jnp.zeros((bs, 1, LANES), F32), pc, "scan_cache")
        lcs_s, bcum_s, bnat_s = _scan_call(g_a[n_p:].reshape(bs, ts, LANES), g_b[n_p:].reshape(bs, ts, LANES),
                                           lcs_c[:, past - 1:past, :], ts, "scan_sample")

        lk_p = (jnp.transpose(lcs_p[:, :, :fh], (0, 2, 1)) * LOG2E).reshape(b * fh, t // tb, 1, tb)
        y_fox = _fox_prompt_call(q_b, k_b, v_b, lk_p, b=b, t=t, fh=fh, dh=dh, tb=tb, nh=nh)
        lk_c = jnp.transpose(lcs_c[:, :, :fh], (0, 2, 1)) * LOG2E
        lk_n = jnp.transpose(lcs_s[:, :, :fh], (0, 2, 1)) * LOG2E
        y_fox = _fox_sample_call(y_fox, q_b, k_b, v_b, cache_fox_k, cache_fox_v, lcs_s.reshape(n_s, LANES),
                                 lk_c, lk_n, layer=l, n_prompt=n_p, bs=bs, ts=ts, fh=fh, dh=dh)

        gain_m = ml_out_g[l].reshape(mh, 1, dv)
        br_p = jnp.transpose(bnat_p[:, :, fh:fh + mh], (0, 2, 1)).reshape(b * mh, t // lp, 1, lp)
        y_ml, c1_p, n1_p, m1_p = _mlstm_call(
            m_q, m_k, m_v, m_o, bcum_p.reshape(n_p, LANES), bnat_p.reshape(n_p, LANES), br_p, gain_m,
            zeros_c, zeros_n, zeros_m, row0=0, nb=b, nc=t // lp, chunk=lp, mh=mh, dqk=dqk, dv=dv, hoff=fh,
            y_prev=None, name="mlstm_prompt")
        br_s = jnp.transpose(bnat_s[:, :, fh:fh + mh], (0, 2, 1)).reshape(bs * mh, 1, 1, ts)
        y_ml, c1_s, n1_s, m1_s = _mlstm_call(
            m_q, m_k, m_v, m_o, bcum_s.reshape(n_s, LANES), bnat_s.reshape(n_s, LANES), br_s, gain_m,
            state_mlstm_c[l], state_mlstm_n[l].reshape(bs, mh, 1, dqk),
            jnp.broadcast_to(state_mlstm_m[l][:, :, None, None], (bs, mh, 1, LANES)),
            row0=n_p, nb=bs, nc=1, chunk=ts, mh=mh, dqk=dqk, dv=dv, hoff=fh, y_prev=y_ml, name="mlstm_sample")

        x1, h2, gates, sel = _merge_call(
            x, g1n, sc1, sh1, gt1, y_fox, y_ml, w_gate[l].astype(BF16), b_gate[l].reshape(1, 2 * d),
            w_out[l].astype(BF16), g2n, sc2, sh2, wr_hi, wr_lo, br, tm=tm_b, tc=tc, grp=grp, n_exp=n_exp, epg=epg)
        moe_out = _moe_sparse(h2, gates, sel, x1, gt2, w_e_gate, w_e_up, w_e_down, layer=l, tmx=tmx, tmc=tmc,
                              grp=grp, n_prompt=n_p, split=l == depth - 1)
        x = moe_out[0]

        outs_p.append((g_a[:n_p, :fh].reshape(b, t, fh), c1_p, n1_p.reshape(b, mh, dqk), m1_p[:, :, 0, 0]))
        outs_s.append((g_a[n_p:, :fh].reshape(bs, ts, fh), c1_s, n1_s.reshape(bs, mh, dqk), m1_s[:, :, 0, 0]))

    def stack(outs, i):
        return jnp.stack([o[i] for o in outs], axis=0)

    k_p, k_s, v_p, v_s = kv_stacks
    return (moe_out[0].reshape(b, t, d), moe_out[1].reshape(bs, ts, d),
            k_p.reshape(depth, b, t, fh, dh), v_p.reshape(depth, b, t, fh, dh),
            stack(outs_p, 0), stack(outs_p, 1), stack(outs_p, 2), stack(outs_p, 3),
            k_s.reshape(depth, bs, ts, fh, dh), v_s.reshape(depth, bs, ts, fh, dh),
            stack(outs_s, 0), stack(outs_s, 1), stack(outs_s, 2), stack(outs_s, 3))
```
